```python
import math
import jax, jax.numpy as jnp
from jax import lax
import numpy as np

D_MODEL = 2048
BATCH = 2
SEQ = 4096
DEPTH = 2
DEC_BATCH = 8
DEC_SEQ = 16
PAST_LEN = 1024

CHUNK = 64
N_EVEN = (DEPTH + 1) // 2
N_ODD = DEPTH // 2
D_A = D_MODEL // 2
D_B = D_MODEL // 2
CONV_A_WIDTH = 31
CONV_B_WIDTH = 3
MLA_HEADS = 16
Q_LORA = 512
KV_LORA = 512
QK_NOPE = 128
QK_ROPE = 64
V_DIM = 128
ROPE_THETA = 10000.0
MLA_SCALE = (QK_NOPE + QK_ROPE) ** -0.5
Q_BLOCK = 128
MEM_LEN = 256
XA_HEADS = 4
XA_HEAD_DIM = D_MODEL // XA_HEADS
PEER_HEADS = 8
PEER_QDIM = 256
PEER_HALF = PEER_QDIM // 2
N_KEYS = 128
N_EXPERTS = N_KEYS * N_KEYS
PEER_TOPK = 16
PEER_BLOCK = 64
EPS = 1e-6

kernel_name = 'hybrid_stream_conv_mla_peer_step'


def _rmsnorm(x, g):
    xf = x.astype(jnp.float32)
    y = xf * lax.rsqrt(jnp.mean(xf * xf, axis=-1, keepdims=True) + EPS)
    return (y * g.astype(jnp.float32)).astype(x.dtype)


def _layernorm(x, g, b):
    xf = x.astype(jnp.float32)
    mu = jnp.mean(xf, axis=-1, keepdims=True)
    xc = xf - mu
    var = jnp.mean(xc * xc, axis=-1, keepdims=True)
    y = xc * lax.rsqrt(var + EPS) * g.astype(jnp.float32) + b.astype(jnp.float32)
    return y.astype(x.dtype)


def _dwconv(x_full, w):
    return lax.conv_general_dilated(
        x_full, w[:, None, :].astype(x_full.dtype), window_strides=(1,), padding='VALID',
        dimension_numbers=('NWC', 'WIO', 'NWC'), feature_group_count=w.shape[-1])


def _conv_mixer(h, ctx_a, ctx_b, w_in, ca_w, ca_b, ln_g, ln_b, cb_w, w_out):
    z = h @ w_in
    a_val, a_gate, b_gate, c_gate, v = jnp.split(
        z, [D_A, 2 * D_A, 2 * D_A + D_B, 2 * D_A + 2 * D_B], axis=-1)
    a_full = jnp.concatenate([ctx_a, a_val * jax.nn.sigmoid(a_gate)], axis=1)
    a = jax.nn.silu(_layernorm(_dwconv(a_full, ca_w) + ca_b, ln_g, ln_b))
    b_full = jnp.concatenate([ctx_b, c_gate * v], axis=1)
    b = b_gate * _dwconv(b_full, cb_w)
    y = jnp.concatenate([a, b], axis=-1) @ w_out
    return y, a_full[:, -(CONV_A_WIDTH - 1):], b_full[:, -(CONV_B_WIDTH - 1):]


def _rope(x, pos):
    inv = 1.0 / (ROPE_THETA ** (jnp.arange(0, QK_ROPE, 2, dtype=jnp.float32) / QK_ROPE))
    ang = pos.astype(jnp.float32)[:, None] * inv[None, :]
    ang = ang.reshape(ang.shape[:1] + (1,) * (x.ndim - 3) + ang.shape[1:])
    cos, sin = jnp.cos(ang), jnp.sin(ang)
    xf = x.astype(jnp.float32)
    x1, x2 = xf[..., :QK_ROPE // 2], xf[..., QK_ROPE // 2:]
    return jnp.concatenate([x1 * cos - x2 * sin, x1 * sin + x2 * cos], axis=-1).astype(x.dtype)


def _mla_project(h, pos, w_down, q_norm, kv_norm, w_uq, w_uk):
    bsz, t, _ = h.shape
    z = h @ w_down
    cq, ckv, kr = jnp.split(z, [Q_LORA, Q_LORA + KV_LORA], axis=-1)
    q = (_rmsnorm(cq, q_norm) @ w_uq).reshape(bsz, t, MLA_HEADS, QK_NOPE + QK_ROPE)
    q_nope, q_rope = q[..., :QK_NOPE], q[..., QK_NOPE:]
    q_lat = jnp.einsum('bthn,chn->bthc', q_nope, w_uk)
    return q_lat, _rope(q_rope, pos), _rmsnorm(ckv, kv_norm), _rope(kr, pos)


def _latent_attn(q_lat, q_rope, q_pos, c, kr, k_pos):
    s = (jnp.einsum('bqhc,bkc->bhqk', q_lat, c)
         + jnp.einsum('bqhr,bkr->bhqk', q_rope, kr)).astype(jnp.float32) * MLA_SCALE
    mask = (k_pos[None, :] // CHUNK) <= (q_pos[:, None] // CHUNK)
    p = jax.nn.softmax(jnp.where(mask[None, None], s, -jnp.inf), axis=-1).astype(c.dtype)
    return jnp.einsum('bhqk,bkc->bqhc', p, c)


def _mla_prompt_attn(q_lat, q_rope, c, kr):
    bsz, t = q_lat.shape[:2]
    nb = t // Q_BLOCK
    pos = jnp.arange(t, dtype=jnp.int32)
    qlb = jnp.moveaxis(q_lat.reshape(bsz, nb, Q_BLOCK, MLA_HEADS, KV_LORA), 1, 0)
    qrb = jnp.moveaxis(q_rope.reshape(bsz, nb, Q_BLOCK, MLA_HEADS, QK_ROPE), 1, 0)
    qpb = pos.reshape(nb, Q_BLOCK)
    o = lax.map(lambda a: _latent_attn(a[0], a[1], a[2], c, kr, pos), (qlb, qrb, qpb))
    return jnp.moveaxis(o, 0, 1).reshape(bsz, t, MLA_HEADS, KV_LORA)


def _mla_out(o_lat, w_uv, w_o):
    bsz, t = o_lat.shape[:2]
    o = jnp.einsum('bthc,chv->bthv', o_lat, w_uv).reshape(bsz, t, MLA_HEADS * V_DIM)
    return o @ w_o


def _mem_kv(mem, g, wk, wv):
    m = _rmsnorm(mem, g)
    bsz, n_mem, _ = m.shape
    k = (m @ wk).reshape(bsz, n_mem, XA_HEADS, XA_HEAD_DIM)
    v = (m @ wv).reshape(bsz, n_mem, XA_HEADS, XA_HEAD_DIM)
    return k, v


def _mem_attn(h, k, v, wq, wo):
    bsz, t, d = h.shape
    q = (h @ wq).reshape(bsz, t, XA_HEADS, XA_HEAD_DIM)
    s = jnp.einsum('bthd,bmhd->bhtm', q, k).astype(jnp.float32) * (XA_HEAD_DIM ** -0.5)
    p = jax.nn.softmax(s, axis=-1).astype(v.dtype)
    o = jnp.einsum('bhtm,bmhd->bthd', p, v).reshape(bsz, t, d)
    return o @ wo


def _peer(h, w_q, subkeys, u_tab, v_tab):
    bsz, t, d = h.shape
    n = bsz * t
    xt = h.reshape(n, d)
    q = (xt @ w_q).reshape(n, PEER_HEADS, 2, PEER_HALF)
    s = jnp.einsum('nhpd,pkd->nhpk', q, subkeys).astype(jnp.float32)
    sv, si = lax.top_k(s, PEER_TOPK)
    cand = (sv[:, :, 0, :, None] + sv[:, :, 1, None, :]).reshape(n, PEER_HEADS, PEER_TOPK * PEER_TOPK)
    cv, ci = lax.top_k(cand, PEER_TOPK)
    e = (jnp.take_along_axis(si[:, :, 0], ci // PEER_TOPK, axis=-1) * N_KEYS
         + jnp.take_along_axis(si[:, :, 1], ci % PEER_TOPK, axis=-1))
    g = jax.nn.softmax(cv, axis=-1).astype(h.dtype)
    nb = -(-n // PEER_BLOCK)
    pad = nb * PEER_BLOCK - n
    xb = jnp.pad(xt, ((0, pad), (0, 0))).reshape(nb, PEER_BLOCK, d)
    eb = jnp.pad(e, ((0, pad), (0, 0), (0, 0))).reshape(nb, PEER_BLOCK, PEER_HEADS, PEER_TOPK)
    gb = jnp.pad(g, ((0, pad), (0, 0), (0, 0))).reshape(nb, PEER_BLOCK, PEER_HEADS, PEER_TOPK)

    def block(a):
        xk, ek, gk = a
        act = jax.nn.gelu(jnp.einsum('nd,nhkd->nhk', xk, jnp.take(u_tab, ek, axis=0)))
        return jnp.einsum('nhk,nhkd->nd', gk * act, jnp.take(v_tab, ek, axis=0))

    out = lax.map(block, (xb, eb, gb)).reshape(nb * PEER_BLOCK, d)[:n]
    return out.reshape(bsz, t, d)


def setup_inputs(seed: int = 0) -> dict:
    key = jax.random.key(seed)
    ks = iter(jax.random.split(key, 48))

    def nrm(shape, scale):
        return jax.random.normal(next(ks), shape, jnp.float32) * scale

    def gain(shape):
        return 1.0 + nrm(shape, 0.01)

    D = D_MODEL
    return {
        'x_prompt': nrm((BATCH, SEQ, D), 1.0),
        'x_sample': nrm((DEC_BATCH, DEC_SEQ, D), 1.0),
        'cache_conv_a': nrm((N_EVEN, DEC_BATCH, CONV_A_WIDTH - 1, D_A), 0.5),
        'cache_conv_b': nrm((N_EVEN, DEC_BATCH, CONV_B_WIDTH - 1, D_B), 0.5),
        'cache_mla_latent': nrm((N_ODD, DEC_BATCH, PAST_LEN, KV_LORA), 1.0),
        'cache_mla_krope': nrm((N_ODD, DEC_BATCH, PAST_LEN, QK_ROPE), 1.0),
        'cache_mem_k': nrm((DEPTH, DEC_BATCH, MEM_LEN, XA_HEADS, XA_HEAD_DIM), 1.0),
        'cache_mem_v': nrm((DEPTH, DEC_BATCH, MEM_LEN, XA_HEADS, XA_HEAD_DIM), 1.0),
        'mem_prompt': nrm((BATCH, MEM_LEN, D), 1.0),
        'norm_mix': gain((DEPTH, D)),
        'w_conv_in': nrm((N_EVEN, D, 2 * D_A + 3 * D_B), D ** -0.5),
        'conv_a_w': nrm((N_EVEN, CONV_A_WIDTH, D_A), CONV_A_WIDTH ** -0.5),
        'conv_a_b': nrm((N_EVEN, D_A), 0.01),
        'ln_a_g': gain((N_EVEN, D_A)),
        'ln_a_b': nrm((N_EVEN, D_A), 0.01),
        'conv_b_w': nrm((N_EVEN, CONV_B_WIDTH, D_B), CONV_B_WIDTH ** -0.5),
        'w_conv_out': nrm((N_EVEN, D_A + D_B, D), (D_A + D_B) ** -0.5),
        'w_mla_down': nrm((N_ODD, D, Q_LORA + KV_LORA + QK_ROPE), D ** -0.5),
        'mla_q_norm': gain((N_ODD, Q_LORA)),
        'mla_kv_norm': gain((N_ODD, KV_LORA)),
        'w_mla_uq': nrm((N_ODD, Q_LORA, MLA_HEADS * (QK_NOPE + QK_ROPE)), Q_LORA ** -0.5),
        'w_mla_uk': nrm((N_ODD, KV_LORA, MLA_HEADS, QK_NOPE), KV_LORA ** -0.5),
        'w_mla_uv': nrm((N_ODD, KV_LORA, MLA_HEADS, V_DIM), KV_LORA ** -0.5),
        'w_mla_o': nrm((N_ODD, MLA_HEADS * V_DIM, D), (MLA_HEADS * V_DIM) ** -0.5),
        'norm_xa': gain((DEPTH, D)),
        'norm_mem': gain((DEPTH, D)),
        'w_xa_q': nrm((DEPTH, D, D), D ** -0.5),
        'w_xa_k': nrm((DEPTH, D, D), D ** -0.5),
        'w_xa_v': nrm((DEPTH, D, D), D ** -0.5),
        'w_xa_o': nrm((DEPTH, D, D), D ** -0.5),
        'norm_ffn': gain((DEPTH, D)),
        'peer_w_q': nrm((DEPTH, D, PEER_HEADS * PEER_QDIM), D ** -0.5),
        'peer_subkeys': nrm((DEPTH, 2, N_KEYS, PEER_HALF), PEER_HALF ** -0.5),
        'peer_u': nrm((DEPTH, N_EXPERTS, D), D ** -0.5),
        'peer_v': nrm((DEPTH, N_EXPERTS, D), PEER_HEADS ** -0.5),
        'norm_final': gain((D,)),
    }


def reference(x_prompt, x_sample, cache_conv_a, cache_conv_b, cache_mla_latent, cache_mla_krope,
              cache_mem_k, cache_mem_v, mem_prompt, norm_mix, w_conv_in, conv_a_w, conv_a_b,
              ln_a_g, ln_a_b, conv_b_w, w_conv_out, w_mla_down, mla_q_norm, mla_kv_norm,
              w_mla_uq, w_mla_uk, w_mla_uv, w_mla_o, norm_xa, norm_mem, w_xa_q, w_xa_k, w_xa_v,
              w_xa_o, norm_ffn, peer_w_q, peer_subkeys, peer_u, peer_v, norm_final):
    xp, xs = x_prompt, x_sample
    bp, tp = xp.shape[:2]
    ts = xs.shape[1]
    past = cache_mla_latent.shape[2]
    pos_p = jnp.arange(tp, dtype=jnp.int32)
    pos_s = past + jnp.arange(ts, dtype=jnp.int32)
    pos_ks = jnp.arange(past + ts, dtype=jnp.int32)
    conv_a_p, conv_b_p, lat_p, kr_p, mem_k_p, mem_v_p = [], [], [], [], [], []
    conv_a_s, conv_b_s, lat_s, kr_s = [], [], [], []
    for l in range(DEPTH):
        j = l // 2
        hp = _rmsnorm(xp, norm_mix[l])
        hs = _rmsnorm(xs, norm_mix[l])
        if l % 2 == 0:
            zero_a = jnp.zeros((bp, CONV_A_WIDTH - 1, D_A), xp.dtype)
            zero_b = jnp.zeros((bp, CONV_B_WIDTH - 1, D_B), xp.dtype)
            dp, sa, sb = _conv_mixer(hp, zero_a, zero_b, w_conv_in[j], conv_a_w[j], conv_a_b[j],
                                     ln_a_g[j], ln_a_b[j], conv_b_w[j], w_conv_out[j])
            conv_a_p.append(sa)
            conv_b_p.append(sb)
            ds, sa, sb = _conv_mixer(hs, cache_conv_a[j], cache_conv_b[j], w_conv_in[j], conv_a_w[j],
                                     conv_a_b[j], ln_a_g[j], ln_a_b[j], conv_b_w[j], w_conv_out[j])
            conv_a_s.append(sa)
            conv_b_s.append(sb)
        else:
            ql, qr, c, kr = _mla_project(hp, pos_p, w_mla_down[j], mla_q_norm[j], mla_kv_norm[j],
                                         w_mla_uq[j], w_mla_uk[j])
            dp = _mla_out(_mla_prompt_attn(ql, qr, c, kr), w_mla_uv[j], w_mla_o[j])
            lat_p.append(c)
            kr_p.append(kr)
            ql, qr, c, kr = _mla_project(hs, pos_s, w_mla_down[j], mla_q_norm[j], mla_kv_norm[j],
                                         w_mla_uq[j], w_mla_uk[j])
            c_all = jnp.concatenate([cache_mla_latent[j], c], axis=1)
            kr_all = jnp.concatenate([cache_mla_krope[j], kr], axis=1)
            ds = _mla_out(_latent_attn(ql, qr, pos_s, c_all, kr_all, pos_ks), w_mla_uv[j], w_mla_o[j])
            lat_s.append(c)
            kr_s.append(kr)
        xp = xp + dp
        xs = xs + ds
        mk, mv = _mem_kv(mem_prompt, norm_mem[l], w_xa_k[l], w_xa_v[l])
        mem_k_p.append(mk)
        mem_v_p.append(mv)
        xp = xp + _mem_attn(_rmsnorm(xp, norm_xa[l]), mk, mv, w_xa_q[l], w_xa_o[l])
        xs = xs + _mem_attn(_rmsnorm(xs, norm_xa[l]), cache_mem_k[l], cache_mem_v[l], w_xa_q[l], w_xa_o[l])
        xp = xp + _peer(_rmsnorm(xp, norm_ffn[l]), peer_w_q[l], peer_subkeys[l], peer_u[l], peer_v[l])
        xs = xs + _peer(_rmsnorm(xs, norm_ffn[l]), peer_w_q[l], peer_subkeys[l], peer_u[l], peer_v[l])
    y_prompt = _rmsnorm(xp, norm_final)
    y_sample = _rmsnorm(xs, norm_final)
    return (y_prompt, y_sample,
            jnp.stack(conv_a_p), jnp.stack(conv_b_p), jnp.stack(lat_p), jnp.stack(kr_p),
            jnp.stack(mem_k_p), jnp.stack(mem_v_p),
            jnp.stack(conv_a_s), jnp.stack(conv_b_s), jnp.stack(lat_s), jnp.stack(kr_s))
```

```python
import functools

import jax
import jax.numpy as jnp
from jax import lax
from jax.experimental import pallas as pl
from jax.experimental.pallas import tpu as pltpu

F32 = jnp.float32
BF16 = jnp.bfloat16

EPS = 1e-6
CHUNK = 64
CONV_A_WIDTH = 31
CONV_B_WIDTH = 3
MLA_HEADS = 16
Q_LORA = 512
KV_LORA = 512
QK_NOPE = 128
QK_ROPE = 64
V_DIM = 128
ROPE_THETA = 10000.0
MLA_SCALE = (QK_NOPE + QK_ROPE) ** -0.5
XA_HEADS = 4
PEER_HEADS = 8
N_KEYS = 128
PEER_TOPK = 16

LANES = 128
NEG_BIG = -1e30
MIB = 1024 * 1024
VMEM_LIMIT = 56 * MIB


def _params(n_axes, vmem=VMEM_LIMIT):
    return pltpu.CompilerParams(dimension_semantics=("arbitrary",) * n_axes,
                                vmem_limit_bytes=vmem)


def _dot_nt(a, b):
    return lax.dot_general(a, b, (((1,), (1,)), ((), ())), preferred_element_type=F32)


def _rms(x, g):
    return x * lax.rsqrt(jnp.mean(x * x, axis=-1, keepdims=True) + EPS) * g


def _nm_body(*refs, has_norm, has_res):
    it = iter(refs)
    x_ref = next(it)
    g_ref = next(it) if has_norm else None
    w_ref = next(it)
    r_ref = next(it) if has_res else None
    o_ref = next(it)
    xn_ref = next(it)

    @pl.when(pl.program_id(1) == 0)
    def _():
        x = x_ref[...].astype(F32)
        if has_norm:
            x = _rms(x, g_ref[...])
        xn_ref[...] = x.astype(BF16)

    acc = jnp.dot(xn_ref[...], w_ref[...].astype(BF16), preferred_element_type=F32)
    if has_res:
        acc = acc + r_ref[...]
    o_ref[...] = acc.astype(o_ref.dtype)


def _norm_matmul(x, w, g=None, res=None, *, x_col_block=0, out_dtype=F32, name):
    m = x.shape[0]
    k, n = w.shape
    tm = min(m, 1024)
    tn = 512 if n % 512 == 0 else n
    has_norm = g is not None
    has_res = res is not None
    in_specs = [pl.BlockSpec((tm, k), lambda i, j: (i, x_col_block))]
    args = [x]
    if has_norm:
        in_specs.append(pl.BlockSpec((1, k), lambda i, j: (0, 0)))
        args.append(g.reshape(1, k))
    in_specs.append(pl.BlockSpec((k, tn), lambda i, j: (0, j)))
    args.append(w)
    if has_res:
        in_specs.append(pl.BlockSpec((tm, tn), lambda i, j: (i, j)))
        args.append(res)
    return pl.pallas_call(
        functools.partial(_nm_body, has_norm=has_norm, has_res=has_res),
        grid=(m // tm, n // tn),
        in_specs=in_specs,
        out_specs=pl.BlockSpec((tm, tn), lambda i, j: (i, j)),
        out_shape=jax.ShapeDtypeStruct((m, n), out_dtype),
        scratch_shapes=[pltpu.VMEM((tm, k), BF16)],
        compiler_params=_params(2),
        name=name,
    )(*args)


def _head_mm_body(a_ref, w_ref, o_ref, *, heads, kin, kout):
    for h in range(heads):
        a = a_ref[:, h * kin:(h + 1) * kin].astype(BF16)
        o_ref[:, h * kout:(h + 1) * kout] = jnp.dot(
            a, w_ref[h].astype(BF16), preferred_element_type=F32).astype(o_ref.dtype)


def _head_matmul(a, w, *, name):
    m = a.shape[0]
    heads, kin, kout = w.shape
    tm = min(m, 512)
    return pl.pallas_call(
        functools.partial(_head_mm_body, heads=heads, kin=kin, kout=kout),
        grid=(m // tm,),
        in_specs=[pl.BlockSpec((tm, heads * kin), lambda i: (i, 0)),
                  pl.BlockSpec((heads, kin, kout), lambda i: (0, 0, 0))],
        out_specs=pl.BlockSpec((tm, heads * kout), lambda i: (i, 0)),
        out_shape=jax.ShapeDtypeStruct((m, heads * kout), BF16),
        compiler_params=_params(1),
        name=name,
    )(a, w)


A_HIST = 32
B_HIST = 8


def _conv_body(av_ref, ag_ref, bg_ref, cg_ref, v_ref, ctxa_ref, ctxb_ref, caw_ref, cab_ref,
               lng_ref, lnb_ref, cbw_ref, y_ref, ta_ref, tb_ref, sa_ref, sb_ref, *, tt, d_a):
    t = pl.program_id(1)

    @pl.when(t == 0)
    def _():
        sa_ref[0:A_HIST, :] = ctxa_ref[...]
        sb_ref[0:B_HIST, :] = ctxb_ref[...]

    @pl.when(t > 0)
    def _():
        sa_ref[0:A_HIST, :] = sa_ref[tt:tt + A_HIST, :]
        sb_ref[0:B_HIST, :] = sb_ref[tt:tt + B_HIST, :]

    sa_ref[A_HIST:A_HIST + tt, :] = av_ref[...] * jax.nn.sigmoid(ag_ref[...])
    sb_ref[B_HIST:B_HIST + tt, :] = cg_ref[...] * v_ref[...]

    off_a = A_HIST - (CONV_A_WIDTH - 1)
    acc = jnp.broadcast_to(cab_ref[...], (tt, d_a))
    for k in range(CONV_A_WIDTH):
        acc = acc + caw_ref[k:k + 1, :] * sa_ref[off_a + k:off_a + k + tt, :]
    mu = jnp.mean(acc, axis=-1, keepdims=True)
    xc = acc - mu
    var = jnp.mean(xc * xc, axis=-1, keepdims=True)
    ln = xc * lax.rsqrt(var + EPS) * lng_ref[...] + lnb_ref[...]
    y_ref[:, 0:d_a] = (ln * jax.nn.sigmoid(ln)).astype(y_ref.dtype)

    off_b = B_HIST - (CONV_B_WIDTH - 1)
    accb = cbw_ref[0:1, :] * sb_ref[off_b:off_b + tt, :]
    for k in range(1, CONV_B_WIDTH):
        accb = accb + cbw_ref[k:k + 1, :] * sb_ref[off_b + k:off_b + k + tt, :]
    y_ref[:, d_a:] = (bg_ref[...] * accb).astype(y_ref.dtype)

    ta_ref[...] = sa_ref[tt:tt + A_HIST, :]
    tb_ref[...] = sb_ref[tt:tt + B_HIST, :]


def _conv_mix(z, ctx_a, ctx_b, ca_w, ca_b, ln_g, ln_b, cb_w):
    nb, t, _ = z.shape
    d_a = ca_w.shape[-1]
    tt = min(t, 256)
    ctx_a = jnp.pad(ctx_a, ((0, 0), (A_HIST - ctx_a.shape[1], 0), (0, 0)))
    ctx_b = jnp.pad(ctx_b, ((0, 0), (B_HIST - ctx_b.shape[1], 0), (0, 0)))
    caw = jnp.pad(ca_w, ((0, A_HIST - CONV_A_WIDTH), (0, 0)))
    cbw = jnp.pad(cb_w, ((0, B_HIST - CONV_B_WIDTH), (0, 0)))

    def zspec(col):
        return pl.BlockSpec((None, tt, d_a), lambda b, i: (b, i, col))

    def vec(rows):
        return pl.BlockSpec((rows, d_a), lambda b, i: (0, 0))

    y, ta, tb = pl.pallas_call(
        functools.partial(_conv_body, tt=tt, d_a=d_a),
        grid=(nb, t // tt),
        in_specs=[zspec(0), zspec(1), zspec(2), zspec(3), zspec(4),
                  pl.BlockSpec((None, A_HIST, d_a), lambda b, i: (b, 0, 0)),
                  pl.BlockSpec((None, B_HIST, d_a), lambda b, i: (b, 0, 0)),
                  vec(A_HIST), vec(1), vec(1), vec(1), vec(B_HIST)],
        out_specs=[pl.BlockSpec((None, tt, 2 * d_a), lambda b, i: (b, i, 0)),
                   pl.BlockSpec((None, A_HIST, d_a), lambda b, i: (b, 0, 0)),
                   pl.BlockSpec((None, B_HIST, d_a), lambda b, i: (b, 0, 0))],
        out_shape=[jax.ShapeDtypeStruct((nb, t, 2 * d_a), BF16),
                   jax.ShapeDtypeStruct((nb, A_HIST, d_a), F32),
                   jax.ShapeDtypeStruct((nb, B_HIST, d_a), F32)],
        scratch_shapes=[pltpu.VMEM((tt + A_HIST, d_a), F32),
                        pltpu.VMEM((tt + B_HIST, d_a), F32)],
        compiler_params=_params(2),
        name="conv_mix",
    )(z, z, z, z, z, ctx_a, ctx_b, caw, ca_b.reshape(1, d_a), ln_g.reshape(1, d_a),
      ln_b.reshape(1, d_a), cbw)
    return (y, ta[:, A_HIST - (CONV_A_WIDTH - 1):], tb[:, B_HIST - (CONV_B_WIDTH - 1):])


def _mla_post_body(z_ref, qr_ref, cos_ref, sin_ref, g_ref, c32_ref, cbf_ref, kr32_ref, krbf_ref,
                   qro_ref):
    z = z_ref[...]
    c = _rms(z[:, Q_LORA:Q_LORA + KV_LORA], g_ref[...])
    c32_ref[...] = c
    cbf_ref[...] = c.astype(BF16)

    cos = cos_ref[...]
    sin = sin_ref[...]
    kr = z[:, Q_LORA + KV_LORA:]
    half = QK_ROPE // 2
    kr_sw = jnp.concatenate([kr[:, half:], kr[:, :half]], axis=1)
    kr_rot = kr * cos[:, :QK_ROPE] + kr_sw * sin[:, :QK_ROPE]
    kr32_ref[...] = kr_rot
    krbf_ref[...] = kr_rot.astype(BF16)

    q = qr_ref[...]
    width = q.shape[1]
    lane = lax.broadcasted_iota(jnp.int32, q.shape, 1)
    first_half = (lane % QK_ROPE) < half
    q_sw = jnp.where(first_half, pltpu.roll(q, width - half, axis=1), pltpu.roll(q, half, axis=1))
    reps = width // LANES
    cos_t = jnp.concatenate([cos] * reps, axis=1)
    sin_t = jnp.concatenate([sin] * reps, axis=1)
    qro_ref[...] = (q * cos_t + q_sw * sin_t).astype(BF16)


def _mla_post(z, q, cos, sin, kv_norm):
    m = z.shape[0]
    tm = min(m, 512)
    rope_w = MLA_HEADS * QK_ROPE
    rope_block = (MLA_HEADS * QK_NOPE) // rope_w
    return pl.pallas_call(
        _mla_post_body,
        grid=(m // tm,),
        in_specs=[pl.BlockSpec((tm, z.shape[1]), lambda i: (i, 0)),
                  pl.BlockSpec((tm, rope_w), lambda i: (i, rope_block)),
                  pl.BlockSpec((tm, LANES), lambda i: (i, 0)),
                  pl.BlockSpec((tm, LANES), lambda i: (i, 0)),
                  pl.BlockSpec((1, KV_LORA), lambda i: (0, 0))],
        out_specs=[pl.BlockSpec((tm, KV_LORA), lambda i: (i, 0)),
                   pl.BlockSpec((tm, KV_LORA), lambda i: (i, 0)),
                   pl.BlockSpec((tm, QK_ROPE), lambda i: (i, 0)),
                   pl.BlockSpec((tm, QK_ROPE), lambda i: (i, 0)),
                   pl.BlockSpec((tm, rope_w), lambda i: (i, 0))],
        out_shape=[jax.ShapeDtypeStruct((m, KV_LORA), F32),
                   jax.ShapeDtypeStruct((m, KV_LORA), BF16),
                   jax.ShapeDtypeStruct((m, QK_ROPE), F32),
                   jax.ShapeDtypeStruct((m, QK_ROPE), BF16),
                   jax.ShapeDtypeStruct((m, rope_w), BF16)],
        compiler_params=_params(1),
        name="mla_post",
    )(z, q, cos, sin, kv_norm.reshape(1, KV_LORA))


def _rope_tables(pos):
    inv = 1.0 / (ROPE_THETA ** (jnp.arange(0, QK_ROPE, 2, dtype=F32) / QK_ROPE))
    ang = pos.astype(F32)[:, None] * inv[None, :]
    cos, sin = jnp.cos(ang), jnp.sin(ang)
    return (jnp.concatenate([cos, cos, cos, cos], axis=1),
            jnp.concatenate([-sin, sin, -sin, sin], axis=1))


MLA_BQ = 128
MLA_BK = 256


def _mla_attn_body(ql_ref, qr_ref, c_ref, kr_ref, o_ref, m_ref, l_ref, acc_ref):
    qi = pl.program_id(1)
    rows = MLA_BQ * MLA_HEADS
    m_ref[...] = jnp.full((rows, 1), NEG_BIG, F32)
    l_ref[...] = jnp.zeros((rows, 1), F32)
    acc_ref[...] = jnp.zeros((rows, KV_LORA), F32)
    ql = ql_ref[...]
    qr = qr_ref[...]
    q0 = qi * MLA_BQ

    def step(kb, masked):
        k0 = pl.multiple_of(kb * MLA_BK, MLA_BK)
        cb = c_ref[pl.ds(k0, MLA_BK), :]
        krb = kr_ref[pl.ds(k0, MLA_BK), :]
        s = (_dot_nt(ql, cb) + _dot_nt(qr, krb)) * MLA_SCALE
        if masked:
            row = lax.broadcasted_iota(jnp.int32, s.shape, 0)
            col = lax.broadcasted_iota(jnp.int32, s.shape, 1)
            q_chunk = (q0 + row // MLA_HEADS) // CHUNK
            k_chunk = (k0 + col) // CHUNK
            s = jnp.where(k_chunk <= q_chunk, s, NEG_BIG)
        m_old = m_ref[...]
        m_new = jnp.maximum(m_old, jnp.max(s, axis=-1, keepdims=True))
        alpha = jnp.exp(m_old - m_new)
        p = jnp.exp(s - m_new)
        l_ref[...] = alpha * l_ref[...] + jnp.sum(p, axis=-1, keepdims=True)
        acc_ref[...] = alpha * acc_ref[...] + jnp.dot(p.astype(BF16), cb,
                                                      preferred_element_type=F32)
        m_ref[...] = m_new

    n_full = q0 // MLA_BK

    def loop_body(kb, carry):
        step(kb, False)
        return carry

    lax.fori_loop(0, n_full, loop_body, 0)
    step(n_full, True)
    o_ref[...] = (acc_ref[...] / l_ref[...]).astype(o_ref.dtype)


def _mla_prompt_attn(qlat, qrope, cbf, krbf, nb, t):
    rows = MLA_BQ * MLA_HEADS
    nq = t // MLA_BQ
    return pl.pallas_call(
        _mla_attn_body,
        grid=(nb, nq),
        in_specs=[pl.BlockSpec((rows, KV_LORA), lambda b, i: (b * nq + i, 0)),
                  pl.BlockSpec((rows, QK_ROPE), lambda b, i: (b * nq + i, 0)),
                  pl.BlockSpec((None, t, KV_LORA), lambda b, i: (b, 0, 0)),
                  pl.BlockSpec((None, t, QK_ROPE), lambda b, i: (b, 0, 0))],
        out_specs=pl.BlockSpec((rows, KV_LORA), lambda b, i: (b * nq + i, 0)),
        out_shape=jax.ShapeDtypeStruct(qlat.shape, BF16),
        scratch_shapes=[pltpu.VMEM((rows, 1), F32), pltpu.VMEM((rows, 1), F32),
                        pltpu.VMEM((rows, KV_LORA), F32)],
        compiler_params=_params(2),
        name="mla_prompt_attn",
    )(qlat, qrope, cbf, krbf)


def _mla_dec_body(ql_ref, qr_ref, cc_ref, ckr_ref, cn_ref, krn_ref, o_ref, *, n_new):
    ql = ql_ref[...]
    qr = qr_ref[...]
    cc = cc_ref[...].astype(BF16)
    cn = cn_ref[...].astype(BF16)
    s1 = (_dot_nt(ql, cc) + _dot_nt(qr, ckr_ref[...].astype(BF16))) * MLA_SCALE
    s2 = (_dot_nt(ql, cn) + _dot_nt(qr, krn_ref[...].astype(BF16))) * MLA_SCALE
    col = lax.broadcasted_iota(jnp.int32, s2.shape, 1)
    s2 = jnp.where(col < n_new, s2, NEG_BIG)
    m = jnp.maximum(jnp.max(s1, axis=-1, keepdims=True), jnp.max(s2, axis=-1, keepdims=True))
    p1 = jnp.exp(s1 - m)
    p2 = jnp.exp(s2 - m)
    l = jnp.sum(p1, axis=-1, keepdims=True) + jnp.sum(p2, axis=-1, keepdims=True)
    o = (jnp.dot(p1.astype(BF16), cc, preferred_element_type=F32)
         + jnp.dot(p2.astype(BF16), cn, preferred_element_type=F32))
    o_ref[...] = (o / l).astype(o_ref.dtype)


def _mla_decode_attn(qlat, qrope, cache_c, cache_kr, c_new, kr_new):
    nb, past, _ = cache_c.shape
    n_new = c_new.shape[1]
    rows = n_new * MLA_HEADS
    c_pad = jnp.pad(c_new, ((0, 0), (0, LANES - n_new), (0, 0)))
    kr_pad = jnp.pad(kr_new, ((0, 0), (0, LANES - n_new), (0, 0)))
    return pl.pallas_call(
        functools.partial(_mla_dec_body, n_new=n_new),
        grid=(nb,),
        in_specs=[pl.BlockSpec((rows, KV_LORA), lambda b: (b, 0)),
                  pl.BlockSpec((rows, QK_ROPE), lambda b: (b, 0)),
                  pl.BlockSpec((None, past, KV_LORA), lambda b: (b, 0, 0)),
                  pl.BlockSpec((None, past, QK_ROPE), lambda b: (b, 0, 0)),
                  pl.BlockSpec((None, LANES, KV_LORA), lambda b: (b, 0, 0)),
                  pl.BlockSpec((None, LANES, QK_ROPE), lambda b: (b, 0, 0))],
        out_specs=pl.BlockSpec((rows, KV_LORA), lambda b: (b, 0)),
        out_shape=jax.ShapeDtypeStruct(qlat.shape, BF16),
        compiler_params=_params(1),
        name="mla_decode_attn",
    )(qlat, qrope, cache_c, cache_kr, c_pad, kr_pad)


def _xattn_body(q_ref, k_ref, v_ref, o_ref, *, heads, hd):
    scale = hd ** -0.5
    for h in range(heads):
        sl = slice(h * hd, (h + 1) * hd)
        q = q_ref[:, sl].astype(BF16)
        k = k_ref[:, sl].astype(BF16)
        v = v_ref[:, sl].astype(BF16)
        s = _dot_nt(q, k) * scale
        e = jnp.exp(s - jnp.max(s, axis=-1, keepdims=True))
        p = e / jnp.sum(e, axis=-1, keepdims=True)
        o_ref[:, sl] = jnp.dot(p.astype(BF16), v, preferred_element_type=F32).astype(o_ref.dtype)


def _xattn(q, k, v):
    nb, t, d = q.shape
    n_mem = k.shape[1]
    tt = min(t, 512)
    return pl.pallas_call(
        functools.partial(_xattn_body, heads=XA_HEADS, hd=d // XA_HEADS),
        grid=(nb, t // tt),
        in_specs=[pl.BlockSpec((None, tt, d), lambda b, i: (b, i, 0)),
                  pl.BlockSpec((None, n_mem, d), lambda b, i: (b, 0, 0)),
                  pl.BlockSpec((None, n_mem, d), lambda b, i: (b, 0, 0))],
        out_specs=pl.BlockSpec((None, tt, d), lambda b, i: (b, i, 0)),
        out_shape=jax.ShapeDtypeStruct((nb, t, d), BF16),
        compiler_params=_params(2),
        name="xattn",
    )(q, k, v)


ROUTE_TN = LANES
STAT_ROWS = 8


def _sorted_top(work, out_ref, n):
    for r in range(n):
        mx = jnp.max(work, axis=0, keepdims=True)
        out_ref[r:r + 1, :] = mx
        if r + 1 < n:
            work = jnp.where(work == mx, -jnp.inf, work)


def _route_body(q_ref, sk_ref, s0_ref, s1_ref, st_ref, a_ref, b_ref, t_ref):
    sk0 = sk_ref[0].astype(BF16)
    sk1 = sk_ref[1].astype(BF16)
    half = sk_ref.shape[2]

    def head(h, carry):
        c0 = pl.multiple_of(h * 2 * half, LANES)
        c1 = pl.multiple_of(h * 2 * half + half, LANES)
        s0 = _dot_nt(sk0, q_ref[:, pl.ds(c0, half)].astype(BF16))
        s1 = _dot_nt(sk1, q_ref[:, pl.ds(c1, half)].astype(BF16))
        s0_ref[h] = s0
        s1_ref[h] = s1
        _sorted_top(s0, a_ref, PEER_TOPK)
        _sorted_top(s1, b_ref, PEER_TOPK)
        a = a_ref[...]
        b = b_ref[...]
        k, g = PEER_TOPK, 8
        pieces = [a[0:1] + b]
        pieces += [a[r:r + 1] + b[0:g] for r in range(1, g)]
        pieces.append(a[g:k] + b[0:1])
        cand = jnp.concatenate(pieces, axis=0)
        _sorted_top(cand, t_ref, PEER_TOPK)
        tau = t_ref[PEER_TOPK - 1:PEER_TOPK, :]
        top = t_ref[0:1, :]
        z = jnp.sum(jnp.where(cand >= tau, jnp.exp(cand - top), 0.0), axis=0, keepdims=True)
        st_ref[h, 0:1, :] = tau
        st_ref[h, 1:2, :] = a[0:1]
        st_ref[h, 2:3, :] = b[0:1]
        st_ref[h, 3:4, :] = 1.0 / z
        st_ref[h, 4:STAT_ROWS, :] = jnp.zeros((STAT_ROWS - 4, ROUTE_TN), F32)
        return carry

    lax.fori_loop(0, PEER_HEADS, head, 0)


def _peer_route(q, subkeys):
    n = q.shape[0]
    half = subkeys.shape[2]
    spec_s = pl.BlockSpec((PEER_HEADS, N_KEYS, ROUTE_TN), lambda i: (0, 0, i))
    return pl.pallas_call(
        _route_body,
        grid=(n // ROUTE_TN,),
        in_specs=[pl.BlockSpec((ROUTE_TN, q.shape[1]), lambda i: (i, 0)),
                  pl.BlockSpec((2, N_KEYS, half), lambda i: (0, 0, 0))],
        out_specs=[spec_s, spec_s,
                   pl.BlockSpec((PEER_HEADS, STAT_ROWS, ROUTE_TN), lambda i: (0, 0, i))],
        out_shape=[jax.ShapeDtypeStruct((PEER_HEADS, N_KEYS, n), F32),
                   jax.ShapeDtypeStruct((PEER_HEADS, N_KEYS, n), F32),
                   jax.ShapeDtypeStruct((PEER_HEADS, STAT_ROWS, n), F32)],
        scratch_shapes=[pltpu.VMEM((PEER_TOPK, ROUTE_TN), F32),
                        pltpu.VMEM((PEER_TOPK, ROUTE_TN), F32),
                        pltpu.VMEM((PEER_TOPK, ROUTE_TN), F32)],
        compiler_params=_params(1),
        name="peer_route",
    )(q, subkeys)


PEER_EB = 512


def _gelu_tanh(x):
    return 0.5 * x * (1.0 + jnp.tanh(0.7978845608028654 * (x + 0.044715 * (x * x * x))))


def _peer_body(x_ref, g_ref, u_ref, vt_ref, s0_ref, s1_ref, st_ref, o_ref,
               xn_ref, e1_ref, acc_ref, act_ref, w_ref, *, tn):
    e = pl.program_id(1)

    @pl.when(e == 0)
    def _():
        xn_ref[...] = _rms(x_ref[...], g_ref[...]).astype(BF16)
        for h in range(PEER_HEADS):
            e1_ref[h] = jnp.exp(s1_ref[h] - st_ref[2, h:h + 1, :])
        acc_ref[...] = jnp.zeros_like(acc_ref)

    act_ref[...] = _dot_nt(u_ref[...].astype(BF16), xn_ref[...])

    groups = PEER_EB // N_KEYS

    def chunk(cidx, carry):
        c0 = pl.multiple_of(cidx * LANES, LANES)
        lanes = pl.ds(c0, LANES)
        for ii in range(groups):
            s0rows = s0_ref[ii, :, lanes]
            thr_all = st_ref[0, :, lanes] - s0rows
            coef_all = jnp.exp(s0rows - st_ref[1, :, lanes]) * st_ref[3, :, lanes]
            g = jnp.zeros((N_KEYS, LANES), F32)
            for h in range(PEER_HEADS):
                g = g + jnp.where(s1_ref[h, :, lanes] >= thr_all[h:h + 1],
                                  e1_ref[h, :, lanes] * coef_all[h:h + 1], 0.0)
            a = act_ref[ii * N_KEYS:(ii + 1) * N_KEYS, pl.ds(c0, LANES)]
            w_ref[ii * N_KEYS:(ii + 1) * N_KEYS, pl.ds(c0, LANES)] = (_gelu_tanh(a) * g).astype(BF16)
        return carry

    lax.fori_loop(0, tn // LANES, chunk, 0)

    acc_ref[...] += jnp.dot(vt_ref[...].astype(BF16), w_ref[...], preferred_element_type=F32)

    @pl.when(e == pl.num_programs(1) - 1)
    def _():
        o_ref[...] = x_ref[...] + acc_ref[...].T


def _peer_dense(x, g, u_tab, vt_tab, s0t, s1t, stats):
    n, d = x.shape
    n_exp = u_tab.shape[0]
    tn = min(n, 512)
    groups = PEER_EB // N_KEYS
    s0_by_key = jnp.transpose(s0t, (1, 0, 2))
    stats_by_kind = jnp.transpose(stats, (1, 0, 2))
    return pl.pallas_call(
        functools.partial(_peer_body, tn=tn),
        grid=(n // tn, n_exp // PEER_EB),
        in_specs=[pl.BlockSpec((tn, d), lambda i, e: (i, 0)),
                  pl.BlockSpec((1, d), lambda i, e: (0, 0)),
                  pl.BlockSpec((PEER_EB, d), lambda i, e: (e, 0)),
                  pl.BlockSpec((d, PEER_EB), lambda i, e: (0, e)),
                  pl.BlockSpec((groups, PEER_HEADS, tn), lambda i, e: (e, 0, i)),
                  pl.BlockSpec((PEER_HEADS, N_KEYS, tn), lambda i, e: (0, 0, i)),
                  pl.BlockSpec((STAT_ROWS, PEER_HEADS, tn), lambda i, e: (0, 0, i))],
        out_specs=pl.BlockSpec((tn, d), lambda i, e: (i, 0)),
        out_shape=jax.ShapeDtypeStruct((n, d), F32),
        scratch_shapes=[pltpu.VMEM((tn, d), BF16),
                        pltpu.VMEM((PEER_HEADS, N_KEYS, tn), F32),
                        pltpu.VMEM((d, tn), F32),
                        pltpu.VMEM((PEER_EB, tn), F32),
                        pltpu.VMEM((PEER_EB, tn), BF16)],
        compiler_params=_params(2),
        name="peer_dense",
    )(x, g.reshape(1, d), u_tab, vt_tab, s0_by_key, s1t, stats_by_kind)


def _final_norm_body(x_ref, g_ref, o_ref):
    o_ref[...] = _rms(x_ref[...], g_ref[...])


def _final_norm(x, g):
    m, d = x.shape
    tm = min(m, 512)
    return pl.pallas_call(
        _final_norm_body,
        grid=(m // tm,),
        in_specs=[pl.BlockSpec((tm, d), lambda i: (i, 0)), pl.BlockSpec((1, d), lambda i: (0, 0))],
        out_specs=pl.BlockSpec((tm, d), lambda i: (i, 0)),
        out_shape=jax.ShapeDtypeStruct((m, d), F32),
        compiler_params=_params(1),
        name="final_norm",
    )(x, g.reshape(1, d))


def _conv_layer(x, nb, t, ctx_a, ctx_b, g, w_in, ca_w, ca_b, ln_g, ln_b, cb_w, w_out):
    z = _norm_matmul(x, w_in, g, name="conv_in")
    y, new_a, new_b = _conv_mix(z.reshape(nb, t, -1), ctx_a, ctx_b, ca_w, ca_b, ln_g, ln_b, cb_w)
    x = _norm_matmul(y.reshape(nb * t, -1), w_out, res=x, name="conv_out")
    return x, new_a, new_b


def _mla_project(x, pos, g, w_down, q_norm, kv_norm, w_uq_perm, w_uk_t):
    z = _norm_matmul(x, w_down, g, name="mla_down")
    q = _norm_matmul(z, w_uq_perm, q_norm, name="mla_uq")
    qlat = _head_matmul(q, w_uk_t, name="mla_qlat")
    cos, sin = _rope_tables(pos)
    c32, cbf, kr32, krbf, qrope = _mla_post(z, q, cos, sin, kv_norm)
    m = x.shape[0]
    return (qlat.reshape(m * MLA_HEADS, KV_LORA), qrope.reshape(m * MLA_HEADS, QK_ROPE),
            c32, cbf, kr32, krbf)


def _mla_finish(x, o_lat, w_uv_t, w_o):
    m = x.shape[0]
    o = _head_matmul(o_lat.reshape(m, MLA_HEADS * KV_LORA), w_uv_t, name="mla_uv")
    return _norm_matmul(o, w_o, res=x, name="mla_o")


def _xattn_layer(x, nb, t, k, v, g, wq, wo):
    d = x.shape[1]
    q = _norm_matmul(x, wq, g, name="xa_q")
    o = _xattn(q.reshape(nb, t, d), k, v)
    return _norm_matmul(o.reshape(nb * t, d), wo, res=x, name="xa_o")


def _peer_layer(x, g, w_q, subkeys, u_tab, vt_tab):
    q = _norm_matmul(x, w_q, g, name="peer_q")
    s0t, s1t, stats = _peer_route(q, subkeys)
    return _peer_dense(x, g, u_tab, vt_tab, s0t, s1t, stats)


def kernel(x_prompt, x_sample, cache_conv_a, cache_conv_b, cache_mla_latent, cache_mla_krope, cache_mem_k, cache_mem_v, mem_prompt, norm_mix, w_conv_in, conv_a_w, conv_a_b, ln_a_g, ln_a_b, conv_b_w, w_conv_out, w_mla_down, mla_q_norm, mla_kv_norm, w_mla_uq, w_mla_uk, w_mla_uv, w_mla_o, norm_xa, norm_mem, w_xa_q, w_xa_k, w_xa_v, w_xa_o, norm_ffn, peer_w_q, peer_subkeys, peer_u, peer_v, norm_final):
    bp, tp, d = x_prompt.shape
    bs, ts, _ = x_sample.shape
    depth = norm_mix.shape[0]
    past = cache_mla_latent.shape[2]
    n_mem = mem_prompt.shape[1]
    xp = x_prompt.reshape(bp * tp, d)
    xs = x_sample.reshape(bs * ts, d)
    pos_p = jnp.tile(jnp.arange(tp, dtype=jnp.int32), bp)
    pos_s = jnp.tile(past + jnp.arange(ts, dtype=jnp.int32), bs)
    mem2d = mem_prompt.reshape(bp * n_mem, d)

    conv_a_p, conv_b_p, lat_p, kr_p, mem_k_p, mem_v_p = [], [], [], [], [], []
    conv_a_s, conv_b_s, lat_s, kr_s = [], [], [], []
    for l in range(depth):
        j = l // 2
        if l % 2 == 0:
            d_a = conv_a_w.shape[-1]
            zero_a = jnp.zeros((bp, CONV_A_WIDTH - 1, d_a), F32)
            zero_b = jnp.zeros((bp, CONV_B_WIDTH - 1, conv_b_w.shape[-1]), F32)
            wts = (norm_mix[l], w_conv_in[j], conv_a_w[j], conv_a_b[j], ln_a_g[j], ln_a_b[j],
                   conv_b_w[j], w_conv_out[j])
            xp, sa, sb = _conv_layer(xp, bp, tp, zero_a, zero_b, *wts)
            conv_a_p.append(sa)
            conv_b_p.append(sb)
            xs, sa, sb = _conv_layer(xs, bs, ts, cache_conv_a[j], cache_conv_b[j], *wts)
            conv_a_s.append(sa)
            conv_b_s.append(sb)
        else:
            uq = w_mla_uq[j].reshape(Q_LORA, MLA_HEADS, QK_NOPE + QK_ROPE)
            w_uq_perm = jnp.concatenate(
                [uq[:, :, :QK_NOPE].reshape(Q_LORA, MLA_HEADS * QK_NOPE),
                 uq[:, :, QK_NOPE:].reshape(Q_LORA, MLA_HEADS * QK_ROPE)], axis=1)
            w_uk_t = jnp.transpose(w_mla_uk[j], (1, 2, 0))
            w_uv_t = jnp.transpose(w_mla_uv[j], (1, 0, 2))
            proj = (norm_mix[l], w_mla_down[j], mla_q_norm[j], mla_kv_norm[j], w_uq_perm, w_uk_t)

            ql, qr, c32, cbf, kr32, krbf = _mla_project(xp, pos_p, *proj)
            o_lat = _mla_prompt_attn(ql, qr, cbf.reshape(bp, tp, KV_LORA),
                                     krbf.reshape(bp, tp, QK_ROPE), bp, tp)
            xp = _mla_finish(xp, o_lat, w_uv_t, w_mla_o[j])
            lat_p.append(c32.reshape(bp, tp, KV_LORA))
            kr_p.append(kr32.reshape(bp, tp, QK_ROPE))

            ql, qr, c32, cbf, kr32, krbf = _mla_project(xs, pos_s, *proj)
            c_new = c32.reshape(bs, ts, KV_LORA)
            kr_new = kr32.reshape(bs, ts, QK_ROPE)
            o_lat = _mla_decode_attn(ql, qr, cache_mla_latent[j], cache_mla_krope[j], c_new, kr_new)
            xs = _mla_finish(xs, o_lat, w_uv_t, w_mla_o[j])
            lat_s.append(c_new)
            kr_s.append(kr_new)

        mk = _norm_matmul(mem2d, w_xa_k[l], norm_mem[l], name="mem_k")
        mv = _norm_matmul(mem2d, w_xa_v[l], norm_mem[l], name="mem_v")
        hd = d // XA_HEADS
        mem_k_p.append(mk.reshape(bp, n_mem, XA_HEADS, hd))
        mem_v_p.append(mv.reshape(bp, n_mem, XA_HEADS, hd))
        xp = _xattn_layer(xp, bp, tp, mk.reshape(bp, n_mem, d), mv.reshape(bp, n_mem, d),
                          norm_xa[l], w_xa_q[l], w_xa_o[l])
        xs = _xattn_layer(xs, bs, ts, cache_mem_k[l].reshape(bs, n_mem, d),
                          cache_mem_v[l].reshape(bs, n_mem, d), norm_xa[l], w_xa_q[l], w_xa_o[l])

        vt_tab = peer_v[l].T
        xp = _peer_layer(xp, norm_ffn[l], peer_w_q[l], peer_subkeys[l], peer_u[l], vt_tab)
        xs = _peer_layer(xs, norm_ffn[l], peer_w_q[l], peer_subkeys[l], peer_u[l], vt_tab)

    y_prompt = _final_norm(xp, norm_final).reshape(bp, tp, d)
    y_sample = _final_norm(xs, norm_final).reshape(bs, ts, d)
    return (y_prompt, y_sample,
            jnp.stack(conv_a_p), jnp.stack(conv_b_p), jnp.stack(lat_p), jnp.stack(kr_p),
            jnp.stack(mem_k_p), jnp.stack(mem_v_p),
            jnp.stack(conv_a_s), jnp.stack(conv_b_s), jnp.stack(lat_s), jnp.stack(kr_s))
```

```python
import functools

import jax
import jax.numpy as jnp
from jax import lax
from jax.experimental import pallas as pl
from jax.experimental.pallas import tpu as pltpu

F32 = jnp.float32
BF16 = jnp.bfloat16

EPS = 1e-6
CHUNK = 64
CONV_A_WIDTH = 31
CONV_B_WIDTH = 3
MLA_HEADS = 16
Q_LORA = 512
KV_LORA = 512
QK_NOPE = 128
QK_ROPE = 64
V_DIM = 128
ROPE_THETA = 10000.0
MLA_SCALE = (QK_NOPE + QK_ROPE) ** -0.5
XA_HEADS = 4
PEER_HEADS = 8
N_KEYS = 128
PEER_TOPK = 16

LANES = 128
NEG_BIG = -1e30
MIB = 1024 * 1024
VMEM_LIMIT = 56 * MIB


def _params(n_axes, vmem=VMEM_LIMIT):
    return pltpu.CompilerParams(dimension_semantics=("arbitrary",) * n_axes,
                                vmem_limit_bytes=vmem)


def _dot_nt(a, b):
    return lax.dot_general(a, b, (((1,), (1,)), ((), ())), preferred_element_type=F32)


def _rms(x, g):
    return x * lax.rsqrt(jnp.mean(x * x, axis=-1, keepdims=True) + EPS) * g


def _nm_body(*refs, has_norm, has_res):
    it = iter(refs)
    x_ref = next(it)
    g_ref = next(it) if has_norm else None
    w_ref = next(it)
    r_ref = next(it) if has_res else None
    o_ref = next(it)
    xn_ref = next(it)

    @pl.when(pl.program_id(1) == 0)
    def _():
        x = x_ref[...].astype(F32)
        if has_norm:
            x = _rms(x, g_ref[...])
        xn_ref[...] = x.astype(BF16)

    acc = jnp.dot(xn_ref[...], w_ref[...].astype(BF16), preferred_element_type=F32)
    if has_res:
        acc = acc + r_ref[...]
    o_ref[...] = acc.astype(o_ref.dtype)


def _norm_matmul(x, w, g=None, res=None, *, layer=None, x_col_block=0, out_dtype=F32, name):
    m = x.shape[0]
    k, n = w.shape[-2:]
    tm = min(m, 1024)
    tn = 512 if n % 512 == 0 else n
    has_norm = g is not None
    has_res = res is not None
    in_specs = [pl.BlockSpec((tm, k), lambda i, j: (i, x_col_block))]
    args = [x]
    if has_norm:
        in_specs.append(pl.BlockSpec((1, k), lambda i, j: (0, 0)))
        args.append(g.reshape(1, k))
    if layer is None:
        in_specs.append(pl.BlockSpec((k, tn), lambda i, j: (0, j)))
    else:
        in_specs.append(pl.BlockSpec((None, k, tn), lambda i, j: (layer, 0, j)))
    args.append(w)
    if has_res:
        in_specs.append(pl.BlockSpec((tm, tn), lambda i, j: (i, j)))
        args.append(res)
    return pl.pallas_call(
        functools.partial(_nm_body, has_norm=has_norm, has_res=has_res),
        grid=(m // tm, n // tn),
        in_specs=in_specs,
        out_specs=pl.BlockSpec((tm, tn), lambda i, j: (i, j)),
        out_shape=jax.ShapeDtypeStruct((m, n), out_dtype),
        scratch_shapes=[pltpu.VMEM((tm, k), BF16)],
        compiler_params=_params(2),
        name=name,
    )(*args)


A_HIST = 32
B_HIST = 8


def _conv_body(av_ref, ag_ref, bg_ref, cg_ref, v_ref, ctxa_ref, ctxb_ref, caw_ref, cab_ref,
               lng_ref, lnb_ref, cbw_ref, y_ref, ta_ref, tb_ref, sa_ref, sb_ref, *, tt, d_a):
    t = pl.program_id(1)

    @pl.when(t == 0)
    def _():
        sa_ref[0:A_HIST, :] = ctxa_ref[...]
        sb_ref[0:B_HIST, :] = ctxb_ref[...]

    @pl.when(t > 0)
    def _():
        sa_ref[0:A_HIST, :] = sa_ref[tt:tt + A_HIST, :]
        sb_ref[0:B_HIST, :] = sb_ref[tt:tt + B_HIST, :]

    sa_ref[A_HIST:A_HIST + tt, :] = av_ref[...] * jax.nn.sigmoid(ag_ref[...])
    sb_ref[B_HIST:B_HIST + tt, :] = cg_ref[...] * v_ref[...]

    off_a = A_HIST - (CONV_A_WIDTH - 1)
    acc = jnp.broadcast_to(cab_ref[...], (tt, d_a))
    for k in range(CONV_A_WIDTH):
        acc = acc + caw_ref[k:k + 1, :] * sa_ref[off_a + k:off_a + k + tt, :]
    mu = jnp.mean(acc, axis=-1, keepdims=True)
    xc = acc - mu
    var = jnp.mean(xc * xc, axis=-1, keepdims=True)
    ln = xc * lax.rsqrt(var + EPS) * lng_ref[...] + lnb_ref[...]
    y_ref[:, 0:d_a] = (ln * jax.nn.sigmoid(ln)).astype(y_ref.dtype)

    off_b = B_HIST - (CONV_B_WIDTH - 1)
    accb = cbw_ref[0:1, :] * sb_ref[off_b:off_b + tt, :]
    for k in range(1, CONV_B_WIDTH):
        accb = accb + cbw_ref[k:k + 1, :] * sb_ref[off_b + k:off_b + k + tt, :]
    y_ref[:, d_a:] = (bg_ref[...] * accb).astype(y_ref.dtype)

    ta_ref[...] = sa_ref[tt:tt + A_HIST, :]
    tb_ref[...] = sb_ref[tt:tt + B_HIST, :]


def _conv_mix(z, ctx_a, ctx_b, ca_w, ca_b, ln_g, ln_b, cb_w):
    nb, t, _ = z.shape
    d_a = ca_w.shape[-1]
    tt = min(t, 256)
    ctx_a = jnp.pad(ctx_a, ((0, 0), (A_HIST - ctx_a.shape[1], 0), (0, 0)))
    ctx_b = jnp.pad(ctx_b, ((0, 0), (B_HIST - ctx_b.shape[1], 0), (0, 0)))
    caw = jnp.pad(ca_w, ((0, A_HIST - CONV_A_WIDTH), (0, 0)))
    cbw = jnp.pad(cb_w, ((0, B_HIST - CONV_B_WIDTH), (0, 0)))

    def zspec(col):
        return pl.BlockSpec((None, tt, d_a), lambda b, i: (b, i, col))

    def vec(rows):
        return pl.BlockSpec((rows, d_a), lambda b, i: (0, 0))

    y, ta, tb = pl.pallas_call(
        functools.partial(_conv_body, tt=tt, d_a=d_a),
        grid=(nb, t // tt),
        in_specs=[zspec(0), zspec(1), zspec(2), zspec(3), zspec(4),
                  pl.BlockSpec((None, A_HIST, d_a), lambda b, i: (b, 0, 0)),
                  pl.BlockSpec((None, B_HIST, d_a), lambda b, i: (b, 0, 0)),
                  vec(A_HIST), vec(1), vec(1), vec(1), vec(B_HIST)],
        out_specs=[pl.BlockSpec((None, tt, 2 * d_a), lambda b, i: (b, i, 0)),
                   pl.BlockSpec((None, A_HIST, d_a), lambda b, i: (b, 0, 0)),
                   pl.BlockSpec((None, B_HIST, d_a), lambda b, i: (b, 0, 0))],
        out_shape=[jax.ShapeDtypeStruct((nb, t, 2 * d_a), BF16),
                   jax.ShapeDtypeStruct((nb, A_HIST, d_a), F32),
                   jax.ShapeDtypeStruct((nb, B_HIST, d_a), F32)],
        scratch_shapes=[pltpu.VMEM((tt + A_HIST, d_a), F32),
                        pltpu.VMEM((tt + B_HIST, d_a), F32)],
        compiler_params=_params(2),
        name="conv_mix",
    )(z, z, z, z, z, ctx_a, ctx_b, caw, ca_b.reshape(1, d_a), ln_g.reshape(1, d_a),
      ln_b.reshape(1, d_a), cbw)
    return (y, ta[:, A_HIST - (CONV_A_WIDTH - 1):], tb[:, B_HIST - (CONV_B_WIDTH - 1):])


def _qlat_body(q_ref, w_ref, o_ref, *, tm):
    for h in range(MLA_HEADS):
        a = q_ref[:, h * QK_NOPE:(h + 1) * QK_NOPE].astype(BF16)
        r = jnp.dot(a, w_ref[h].astype(BF16), preferred_element_type=F32).astype(BF16)
        for j in range(tm // CHUNK):
            o_ref[j, h] = r[j * CHUNK:(j + 1) * CHUNK]


def _mla_qlat(q, w_uk_t):
    m = q.shape[0]
    tm = min(m, 512)
    return pl.pallas_call(
        functools.partial(_qlat_body, tm=tm),
        grid=(m // tm,),
        in_specs=[pl.BlockSpec((tm, MLA_HEADS * QK_NOPE), lambda i: (i, 0)),
                  pl.BlockSpec((MLA_HEADS, QK_NOPE, KV_LORA), lambda i: (0, 0, 0))],
        out_specs=pl.BlockSpec((tm // CHUNK, MLA_HEADS, CHUNK, KV_LORA), lambda i: (i, 0, 0, 0)),
        out_shape=jax.ShapeDtypeStruct((m // CHUNK, MLA_HEADS, CHUNK, KV_LORA), BF16),
        compiler_params=_params(1),
        name="mla_qlat",
    )(q, w_uk_t)


def _mla_post_body(z_ref, qr_ref, cos_ref, sin_ref, g_ref, c32_ref, cbf_ref, kr32_ref, krbf_ref,
                   qro_ref, *, tm):
    z = z_ref[...]
    c = _rms(z[:, Q_LORA:Q_LORA + KV_LORA], g_ref[...])
    c32_ref[...] = c
    cbf_ref[...] = c.astype(BF16)

    cos = cos_ref[...]
    sin = sin_ref[...]
    kr = z[:, Q_LORA + KV_LORA:]
    half = QK_ROPE // 2
    kr_sw = jnp.concatenate([kr[:, half:], kr[:, :half]], axis=1)
    kr_rot = kr * cos[:, :QK_ROPE] + kr_sw * sin[:, :QK_ROPE]
    kr32_ref[...] = kr_rot
    krbf_ref[...] = kr_rot.astype(BF16)

    q = qr_ref[...]
    width = q.shape[1]
    lane = lax.broadcasted_iota(jnp.int32, q.shape, 1)
    first_half = (lane % QK_ROPE) < half
    q_sw = jnp.where(first_half, pltpu.roll(q, width - half, axis=1), pltpu.roll(q, half, axis=1))
    reps = width // LANES
    cos_t = jnp.concatenate([cos] * reps, axis=1)
    sin_t = jnp.concatenate([sin] * reps, axis=1)
    q_rot = q * cos_t + q_sw * sin_t
    for h in range(MLA_HEADS):
        piece = q_rot[:, h * QK_ROPE:(h + 1) * QK_ROPE].astype(BF16)
        for j in range(tm // CHUNK):
            qro_ref[j, h] = piece[j * CHUNK:(j + 1) * CHUNK]


def _mla_post(z, q, cos, sin, kv_norm):
    m = z.shape[0]
    tm = min(m, 512)
    rope_w = MLA_HEADS * QK_ROPE
    rope_block = (MLA_HEADS * QK_NOPE) // rope_w
    return pl.pallas_call(
        functools.partial(_mla_post_body, tm=tm),
        grid=(m // tm,),
        in_specs=[pl.BlockSpec((tm, z.shape[1]), lambda i: (i, 0)),
                  pl.BlockSpec((tm, rope_w), lambda i: (i, rope_block)),
                  pl.BlockSpec((tm, LANES), lambda i: (i, 0)),
                  pl.BlockSpec((tm, LANES), lambda i: (i, 0)),
                  pl.BlockSpec((1, KV_LORA), lambda i: (0, 0))],
        out_specs=[pl.BlockSpec((tm, KV_LORA), lambda i: (i, 0)),
                   pl.BlockSpec((tm, KV_LORA), lambda i: (i, 0)),
                   pl.BlockSpec((tm, QK_ROPE), lambda i: (i, 0)),
                   pl.BlockSpec((tm, QK_ROPE), lambda i: (i, 0)),
                   pl.BlockSpec((tm // CHUNK, MLA_HEADS, CHUNK, QK_ROPE), lambda i: (i, 0, 0, 0))],
        out_shape=[jax.ShapeDtypeStruct((m, KV_LORA), F32),
                   jax.ShapeDtypeStruct((m, KV_LORA), BF16),
                   jax.ShapeDtypeStruct((m, QK_ROPE), F32),
                   jax.ShapeDtypeStruct((m, QK_ROPE), BF16),
                   jax.ShapeDtypeStruct((m // CHUNK, MLA_HEADS, CHUNK, QK_ROPE), BF16)],
        compiler_params=_params(1),
        name="mla_post",
    )(z, q, cos, sin, kv_norm.reshape(1, KV_LORA))


def _rope_tables(pos):
    inv = 1.0 / (ROPE_THETA ** (jnp.arange(0, QK_ROPE, 2, dtype=F32) / QK_ROPE))
    ang = pos.astype(F32)[:, None] * inv[None, :]
    cos, sin = jnp.cos(ang), jnp.sin(ang)
    return (jnp.concatenate([cos, cos, cos, cos], axis=1),
            jnp.concatenate([-sin, sin, -sin, sin], axis=1))


MLA_KB = 512


def _mla_attn_body(ql_ref, qr_ref, c_ref, kr_ref, o_ref, s_ref, m_ref, l_ref, acc_ref):
    ci = pl.program_id(1)
    rows = MLA_HEADS * CHUNK
    ql = ql_ref[...].reshape(rows, KV_LORA)
    qr = qr_ref[...].reshape(rows, QK_ROPE)
    n_valid = (ci + 1) * CHUNK
    n_full = n_valid // MLA_KB
    tiles = MLA_KB // LANES
    m_ref[...] = jnp.full((rows, LANES), NEG_BIG, F32)

    def scores(kb, n_keep):
        k0 = pl.multiple_of(kb * MLA_KB, MLA_KB)
        s = (_dot_nt(ql, c_ref[pl.ds(k0, MLA_KB), :])
             + _dot_nt(qr, kr_ref[pl.ds(k0, MLA_KB), :])) * MLA_SCALE
        if n_keep is not None:
            col = lax.broadcasted_iota(jnp.int32, s.shape, 1)
            s = jnp.where(col < n_keep, s, NEG_BIG)
        s_ref[kb] = s
        mp = m_ref[...]
        for t in range(tiles):
            mp = jnp.maximum(mp, s[:, t * LANES:(t + 1) * LANES])
        m_ref[...] = mp

    def pass1(kb, carry):
        scores(kb, None)
        return carry

    lax.fori_loop(0, n_full, pass1, 0)
    rem = n_valid - n_full * MLA_KB

    @pl.when(rem > 0)
    def _():
        scores(n_full, rem)

    m = jnp.max(m_ref[...], axis=-1, keepdims=True)
    m_ref[...] = jnp.broadcast_to(m, (rows, LANES))
    l_ref[...] = jnp.zeros((rows, LANES), F32)
    acc_ref[...] = jnp.zeros((rows, KV_LORA), F32)

    def pass2(kb, carry):
        k0 = pl.multiple_of(kb * MLA_KB, MLA_KB)
        m_rep = m_ref[...]
        p = jnp.exp(s_ref[kb] - jnp.concatenate([m_rep] * tiles, axis=1))
        lp = l_ref[...]
        for t in range(tiles):
            lp = lp + p[:, t * LANES:(t + 1) * LANES]
        l_ref[...] = lp
        acc_ref[...] += jnp.dot(p.astype(BF16), c_ref[pl.ds(k0, MLA_KB), :],
                                preferred_element_type=F32)
        return carry

    lax.fori_loop(0, (n_valid + MLA_KB - 1) // MLA_KB, pass2, 0)
    l = jnp.sum(l_ref[...], axis=-1, keepdims=True)
    o_ref[...] = (acc_ref[...] / l).astype(o_ref.dtype).reshape(MLA_HEADS, CHUNK, KV_LORA)


def _mla_prompt_attn(qlat, qrope, cbf, krbf):
    nb, t, _ = cbf.shape
    nc = t // CHUNK
    rows = MLA_HEADS * CHUNK
    return pl.pallas_call(
        _mla_attn_body,
        grid=(nb, nc),
        in_specs=[pl.BlockSpec((None, MLA_HEADS, CHUNK, KV_LORA), lambda b, i: (b * nc + i, 0, 0, 0)),
                  pl.BlockSpec((None, MLA_HEADS, CHUNK, QK_ROPE), lambda b, i: (b * nc + i, 0, 0, 0)),
                  pl.BlockSpec((None, t, KV_LORA), lambda b, i: (b, 0, 0)),
                  pl.BlockSpec((None, t, QK_ROPE), lambda b, i: (b, 0, 0))],
        out_specs=pl.BlockSpec((None, MLA_HEADS, CHUNK, KV_LORA), lambda b, i: (b * nc + i, 0, 0, 0)),
        out_shape=jax.ShapeDtypeStruct(qlat.shape, BF16),
        scratch_shapes=[pltpu.VMEM((t // MLA_KB, rows, MLA_KB), F32),
                        pltpu.VMEM((rows, LANES), F32), pltpu.VMEM((rows, LANES), F32),
                        pltpu.VMEM((rows, KV_LORA), F32)],
        compiler_params=_params(2),
        name="mla_prompt_attn",
    )(qlat, qrope, cbf, krbf)


def _mla_dec_body(ql_ref, qr_ref, cc_ref, ckr_ref, cn_ref, krn_ref, o_ref, *, n_new):
    rows = MLA_HEADS * n_new
    ql = ql_ref[...].reshape(rows, KV_LORA)
    qr = qr_ref[...].reshape(rows, QK_ROPE)
    cc = cc_ref[...].astype(BF16)
    cn = cn_ref[...].astype(BF16)
    s1 = (_dot_nt(ql, cc) + _dot_nt(qr, ckr_ref[...].astype(BF16))) * MLA_SCALE
    s2 = (_dot_nt(ql, cn) + _dot_nt(qr, krn_ref[...].astype(BF16))) * MLA_SCALE
    col = lax.broadcasted_iota(jnp.int32, s2.shape, 1)
    s2 = jnp.where(col < n_new, s2, NEG_BIG)
    m = jnp.maximum(jnp.max(s1, axis=-1, keepdims=True), jnp.max(s2, axis=-1, keepdims=True))
    p1 = jnp.exp(s1 - m)
    p2 = jnp.exp(s2 - m)
    l = jnp.sum(p1, axis=-1, keepdims=True) + jnp.sum(p2, axis=-1, keepdims=True)
    o = (jnp.dot(p1.astype(BF16), cc, preferred_element_type=F32)
         + jnp.dot(p2.astype(BF16), cn, preferred_element_type=F32))
    o_ref[...] = (o / l).astype(o_ref.dtype).reshape(MLA_HEADS, n_new, KV_LORA)


def _mla_decode_attn(qlat, qrope, cache_c, cache_kr, c_new, kr_new):
    nb, past, _ = cache_c.shape
    n_new = c_new.shape[1]
    per_chunk = CHUNK // n_new
    c_pad = jnp.pad(c_new, ((0, 0), (0, LANES - n_new), (0, 0)))
    kr_pad = jnp.pad(kr_new, ((0, 0), (0, LANES - n_new), (0, 0)))

    def qspec(width):
        return pl.BlockSpec((None, MLA_HEADS, n_new, width),
                            lambda b: (b // per_chunk, 0, b % per_chunk, 0))

    return pl.pallas_call(
        functools.partial(_mla_dec_body, n_new=n_new),
        grid=(nb,),
        in_specs=[qspec(KV_LORA), qspec(QK_ROPE),
                  pl.BlockSpec((None, past, KV_LORA), lambda b: (b, 0, 0)),
                  pl.BlockSpec((None, past, QK_ROPE), lambda b: (b, 0, 0)),
                  pl.BlockSpec((None, LANES, KV_LORA), lambda b: (b, 0, 0)),
                  pl.BlockSpec((None, LANES, QK_ROPE), lambda b: (b, 0, 0))],
        out_specs=qspec(KV_LORA),
        out_shape=jax.ShapeDtypeStruct(qlat.shape, BF16),
        compiler_params=_params(1),
        name="mla_decode_attn",
    )(qlat, qrope, cache_c, cache_kr, c_pad, kr_pad)


def _uv_body(o_ref, w_ref, out_ref, *, tm):
    for h in range(MLA_HEADS):
        a = jnp.concatenate([o_ref[j, h] for j in range(tm // CHUNK)], axis=0)
        out_ref[:, h * V_DIM:(h + 1) * V_DIM] = jnp.dot(
            a, w_ref[h].astype(BF16), preferred_element_type=F32).astype(out_ref.dtype)


def _mla_uv(o_lat, w_uv_t):
    m = o_lat.shape[0] * CHUNK
    tm = min(m, 512)
    return pl.pallas_call(
        functools.partial(_uv_body, tm=tm),
        grid=(m // tm,),
        in_specs=[pl.BlockSpec((tm // CHUNK, MLA_HEADS, CHUNK, KV_LORA), lambda i: (i, 0, 0, 0)),
                  pl.BlockSpec((MLA_HEADS, KV_LORA, V_DIM), lambda i: (0, 0, 0))],
        out_specs=pl.BlockSpec((tm, MLA_HEADS * V_DIM), lambda i: (i, 0)),
        out_shape=jax.ShapeDtypeStruct((m, MLA_HEADS * V_DIM), BF16),
        compiler_params=_params(1),
        name="mla_uv",
    )(o_lat, w_uv_t)


def _xattn_body(q_ref, k_ref, v_ref, o_ref, *, heads, hd):
    scale = hd ** -0.5
    for h in range(heads):
        sl = slice(h * hd, (h + 1) * hd)
        q = q_ref[:, sl].astype(BF16)
        k = k_ref[:, sl].astype(BF16)
        v = v_ref[:, sl].astype(BF16)
        s = _dot_nt(q, k) * scale
        e = jnp.exp(s - jnp.max(s, axis=-1, keepdims=True))
        p = e / jnp.sum(e, axis=-1, keepdims=True)
        o_ref[:, sl] = jnp.dot(p.astype(BF16), v, preferred_element_type=F32).astype(o_ref.dtype)


def _xattn(q, k, v):
    nb, t, d = q.shape
    n_mem = k.shape[1]
    tt = min(t, 512)
    return pl.pallas_call(
        functools.partial(_xattn_body, heads=XA_HEADS, hd=d // XA_HEADS),
        grid=(nb, t // tt),
        in_specs=[pl.BlockSpec((None, tt, d), lambda b, i: (b, i, 0)),
                  pl.BlockSpec((None, n_mem, d), lambda b, i: (b, 0, 0)),
                  pl.BlockSpec((None, n_mem, d), lambda b, i: (b, 0, 0))],
        out_specs=pl.BlockSpec((None, tt, d), lambda b, i: (b, i, 0)),
        out_shape=jax.ShapeDtypeStruct((nb, t, d), BF16),
        compiler_params=_params(2),
        name="xattn",
    )(q, k, v)


ROUTE_TN = LANES
STAT_ROWS = 8


def _sorted_top_pair(w0, w1, n, rows_out):
    rows0, rows1 = [], []
    for r in range(n):
        m0 = jnp.max(w0, axis=0, keepdims=True)
        m1 = jnp.max(w1, axis=0, keepdims=True)
        rows0.append(m0)
        rows1.append(m1)
        if r + 1 < n:
            w0 = jnp.where(w0 == m0, -jnp.inf, w0)
            w1 = jnp.where(w1 == m1, -jnp.inf, w1)
    pad = [jnp.full((rows_out - n, w0.shape[1]), -jnp.inf, F32)]
    return jnp.concatenate(rows0 + pad, axis=0), jnp.concatenate(rows1 + pad, axis=0)


ROUTE_N = PEER_TOPK + 1
ROUTE_ROWS = 24


def _pair_sum_candidates(a, b):
    g = 8
    pieces = [a[0:1] + b]
    pieces += [a[r:r + 1] + b[0:g] for r in range(1, g)]
    pieces.append(a[g:ROUTE_ROWS] + b[0:1])
    return jnp.concatenate(pieces, axis=0)


def _route_body(q_ref, sk_ref, s0_ref, s1_ref, st_ref):
    sk0 = sk_ref[0].astype(BF16)
    sk1 = sk_ref[1].astype(BF16)
    half = sk_ref.shape[2]

    def head_lists(h):
        c0 = pl.multiple_of(h * 2 * half, LANES)
        c1 = pl.multiple_of(h * 2 * half + half, LANES)
        s0 = _dot_nt(sk0, q_ref[:, pl.ds(c0, half)].astype(BF16))
        s1 = _dot_nt(sk1, q_ref[:, pl.ds(c1, half)].astype(BF16))
        s0_ref[h] = s0
        s1_ref[h] = s1
        return _sorted_top_pair(s0, s1, ROUTE_N, ROUTE_ROWS)

    def write_stats(h, a, b, cand, top):
        tau = 0.5 * (top[PEER_TOPK - 1:PEER_TOPK] + top[PEER_TOPK:PEER_TOPK + 1])
        z = jnp.sum(jnp.where(cand >= tau, jnp.exp(cand - top[0:1]), 0.0), axis=0, keepdims=True)
        st_ref[h] = jnp.concatenate(
            [tau, a[0:1], b[0:1], 1.0 / z, jnp.zeros((STAT_ROWS - 4, ROUTE_TN), F32)], axis=0)

    def head_pair(hp, carry):
        h0 = hp * 2
        a0, b0 = head_lists(h0)
        a1, b1 = head_lists(h0 + 1)
        cand0 = _pair_sum_candidates(a0, b0)
        cand1 = _pair_sum_candidates(a1, b1)
        top0, top1 = _sorted_top_pair(cand0, cand1, ROUTE_N, ROUTE_ROWS)
        write_stats(h0, a0, b0, cand0, top0)
        write_stats(h0 + 1, a1, b1, cand1, top1)
        return carry

    lax.fori_loop(0, PEER_HEADS // 2, head_pair, 0)


def _peer_route(q, subkeys):
    n = q.shape[0]
    half = subkeys.shape[2]
    spec_s = pl.BlockSpec((PEER_HEADS, N_KEYS, ROUTE_TN), lambda i: (0, 0, i))
    return pl.pallas_call(
        _route_body,
        grid=(n // ROUTE_TN,),
        in_specs=[pl.BlockSpec((ROUTE_TN, q.shape[1]), lambda i: (i, 0)),
                  pl.BlockSpec((2, N_KEYS, half), lambda i: (0, 0, 0))],
        out_specs=[spec_s, spec_s,
                   pl.BlockSpec((PEER_HEADS, STAT_ROWS, ROUTE_TN), lambda i: (0, 0, i))],
        out_shape=[jax.ShapeDtypeStruct((PEER_HEADS, N_KEYS, n), F32),
                   jax.ShapeDtypeStruct((PEER_HEADS, N_KEYS, n), F32),
                   jax.ShapeDtypeStruct((PEER_HEADS, STAT_ROWS, n), F32)],
        compiler_params=_params(1),
        name="peer_route",
    )(q, subkeys)


PEER_EB = 512


def _gelu_tanh(x):
    return 0.5 * x * (1.0 + jnp.tanh(0.7978845608028654 * (x + 0.044715 * (x * x * x))))


def _peer_body(x_ref, g_ref, u_ref, v_ref, s0_ref, s1_ref, st_ref, o_ref, xn_ref, e1_ref, w_ref,
               *, tn):
    @pl.when(pl.program_id(1) == 0)
    def _():
        xn_ref[...] = _rms(x_ref[...], g_ref[...]).astype(BF16)
        for h in range(PEER_HEADS):
            e1_ref[h] = jnp.exp(s1_ref[h] - st_ref[2, h:h + 1, :])
        o_ref[...] = x_ref[...]

    act = _dot_nt(u_ref[...], xn_ref[...])
    for c in range(tn // LANES):
        lanes = slice(c * LANES, (c + 1) * LANES)
        for ii in range(PEER_EB // N_KEYS):
            s0rows = s0_ref[ii, :, lanes]
            thr_all = st_ref[0, :, lanes] - s0rows
            coef_all = jnp.exp(s0rows - st_ref[1, :, lanes]) * st_ref[3, :, lanes]
            g = None
            for h in range(PEER_HEADS):
                gh = jnp.where(s1_ref[h, :, lanes] >= thr_all[h:h + 1],
                               e1_ref[h, :, lanes] * coef_all[h:h + 1], 0.0)
                g = gh if g is None else g + gh
            a = act[ii * N_KEYS:(ii + 1) * N_KEYS, lanes]
            w_ref[ii * N_KEYS:(ii + 1) * N_KEYS, lanes] = (_gelu_tanh(a) * g).astype(BF16)
    o_ref[...] += lax.dot_general(w_ref[...], v_ref[...], (((0,), (0,)), ((), ())),
                                  preferred_element_type=F32)


def _peer_dense(x, g, u_tab, v_tab, s0t, s1t, stats):
    n, d = x.shape
    n_exp = u_tab.shape[0]
    tn = min(n, 512)
    groups = PEER_EB // N_KEYS
    s0_by_key = jnp.transpose(s0t, (1, 0, 2))
    stats_by_kind = jnp.transpose(stats, (1, 0, 2))
    return pl.pallas_call(
        functools.partial(_peer_body, tn=tn),
        grid=(n // tn, n_exp // PEER_EB),
        in_specs=[pl.BlockSpec((tn, d), lambda i, e: (i, 0)),
                  pl.BlockSpec((1, d), lambda i, e: (0, 0)),
                  pl.BlockSpec((PEER_EB, d), lambda i, e: (e, 0)),
                  pl.BlockSpec((PEER_EB, d), lambda i, e: (e, 0)),
                  pl.BlockSpec((groups, PEER_HEADS, tn), lambda i, e: (e, 0, i)),
                  pl.BlockSpec((PEER_HEADS, N_KEYS, tn), lambda i, e: (0, 0, i)),
                  pl.BlockSpec((STAT_ROWS, PEER_HEADS, tn), lambda i, e: (0, 0, i))],
        out_specs=pl.BlockSpec((tn, d), lambda i, e: (i, 0)),
        out_shape=jax.ShapeDtypeStruct((n, d), F32),
        scratch_shapes=[pltpu.VMEM((tn, d), BF16),
                        pltpu.VMEM((PEER_HEADS, N_KEYS, tn), F32),
                        pltpu.VMEM((PEER_EB, tn), BF16)],
        compiler_params=_params(2),
        name="peer_dense",
    )(x, g.reshape(1, d), u_tab, v_tab, s0_by_key, s1t, stats_by_kind)


def _final_norm_body(x_ref, g_ref, o_ref):
    o_ref[...] = _rms(x_ref[...], g_ref[...])


def _final_norm(x, g):
    m, d = x.shape
    tm = min(m, 512)
    return pl.pallas_call(
        _final_norm_body,
        grid=(m // tm,),
        in_specs=[pl.BlockSpec((tm, d), lambda i: (i, 0)), pl.BlockSpec((1, d), lambda i: (0, 0))],
        out_specs=pl.BlockSpec((tm, d), lambda i: (i, 0)),
        out_shape=jax.ShapeDtypeStruct((m, d), F32),
        compiler_params=_params(1),
        name="final_norm",
    )(x, g.reshape(1, d))


def _conv_layer(x, nb, t, ctx_a, ctx_b, g, w_in, ca_w, ca_b, ln_g, ln_b, cb_w, w_out):
    z = _norm_matmul(x, w_in, g, name="conv_in")
    y, new_a, new_b = _conv_mix(z.reshape(nb, t, -1), ctx_a, ctx_b, ca_w, ca_b, ln_g, ln_b, cb_w)
    x = _norm_matmul(y.reshape(nb * t, -1), w_out, res=x, name="conv_out")
    return x, new_a, new_b


def _mla_project(x, pos, g, w_down, q_norm, kv_norm, w_uq_perm, w_uk_t):
    z = _norm_matmul(x, w_down, g, name="mla_down")
    q = _norm_matmul(z, w_uq_perm, q_norm, name="mla_uq")
    qlat = _mla_qlat(q, w_uk_t)
    cos, sin = _rope_tables(pos)
    c32, cbf, kr32, krbf, qrope = _mla_post(z, q, cos, sin, kv_norm)
    return qlat, qrope, c32, cbf, kr32, krbf


def _mla_finish(x, o_lat, w_uv_t, w_o):
    return _norm_matmul(_mla_uv(o_lat, w_uv_t), w_o, res=x, name="mla_o")


def _xattn_layer(x, nb, t, k, v, g, wq, wo, layer):
    d = x.shape[1]
    q = _norm_matmul(x, wq, g, layer=layer, name="xa_q")
    o = _xattn(q.reshape(nb, t, d), k, v)
    return _norm_matmul(o.reshape(nb * t, d), wo, res=x, layer=layer, name="xa_o")


def _peer_layer(x, g, w_q, layer, subkeys, u_tab, v_tab):
    q = _norm_matmul(x, w_q, g, layer=layer, name="peer_q")
    s0t, s1t, stats = _peer_route(q, subkeys)
    return _peer_dense(x, g, u_tab, v_tab, s0t, s1t, stats)


def kernel(x_prompt, x_sample, cache_conv_a, cache_conv_b, cache_mla_latent, cache_mla_krope, cache_mem_k, cache_mem_v, mem_prompt, norm_mix, w_conv_in, conv_a_w, conv_a_b, ln_a_g, ln_a_b, conv_b_w, w_conv_out, w_mla_down, mla_q_norm, mla_kv_norm, w_mla_uq, w_mla_uk, w_mla_uv, w_mla_o, norm_xa, norm_mem, w_xa_q, w_xa_k, w_xa_v, w_xa_o, norm_ffn, peer_w_q, peer_subkeys, peer_u, peer_v, norm_final):
    bp, tp, d = x_prompt.shape
    bs, ts, _ = x_sample.shape
    depth = norm_mix.shape[0]
    past = cache_mla_latent.shape[2]
    n_mem = mem_prompt.shape[1]
    xp = x_prompt.reshape(bp * tp, d)
    xs = x_sample.reshape(bs * ts, d)
    pos_p = jnp.tile(jnp.arange(tp, dtype=jnp.int32), bp)
    pos_s = jnp.tile(past + jnp.arange(ts, dtype=jnp.int32), bs)
    mem2d = mem_prompt.reshape(bp * n_mem, d)

    conv_a_p, conv_b_p, lat_p, kr_p, mem_k_p, mem_v_p = [], [], [], [], [], []
    conv_a_s, conv_b_s, lat_s, kr_s = [], [], [], []
    for l in range(depth):
        j = l // 2
        if l % 2 == 0:
            d_a = conv_a_w.shape[-1]
            zero_a = jnp.zeros((bp, CONV_A_WIDTH - 1, d_a), F32)
            zero_b = jnp.zeros((bp, CONV_B_WIDTH - 1, conv_b_w.shape[-1]), F32)
            wts = (norm_mix[l], w_conv_in[j], conv_a_w[j], conv_a_b[j], ln_a_g[j], ln_a_b[j],
                   conv_b_w[j], w_conv_out[j])
            xp, sa, sb = _conv_layer(xp, bp, tp, zero_a, zero_b, *wts)
            conv_a_p.append(sa)
            conv_b_p.append(sb)
            xs, sa, sb = _conv_layer(xs, bs, ts, cache_conv_a[j], cache_conv_b[j], *wts)
            conv_a_s.append(sa)
            conv_b_s.append(sb)
        else:
            uq = w_mla_uq[j].reshape(Q_LORA, MLA_HEADS, QK_NOPE + QK_ROPE)
            w_uq_perm = jnp.concatenate(
                [uq[:, :, :QK_NOPE].reshape(Q_LORA, MLA_HEADS * QK_NOPE),
                 uq[:, :, QK_NOPE:].reshape(Q_LORA, MLA_HEADS * QK_ROPE)], axis=1)
            w_uk_t = jnp.transpose(w_mla_uk[j], (1, 2, 0))
            w_uv_t = jnp.transpose(w_mla_uv[j], (1, 0, 2))
            proj = (norm_mix[l], w_mla_down[j], mla_q_norm[j], mla_kv_norm[j], w_uq_perm, w_uk_t)

            ql, qr, c32, cbf, kr32, krbf = _mla_project(xp, pos_p, *proj)
            o_lat = _mla_prompt_attn(ql, qr, cbf.reshape(bp, tp, KV_LORA),
                                     krbf.reshape(bp, tp, QK_ROPE))
            xp = _mla_finish(xp, o_lat, w_uv_t, w_mla_o[j])
            lat_p.append(c32.reshape(bp, tp, KV_LORA))
            kr_p.append(kr32.reshape(bp, tp, QK_ROPE))

            ql, qr, c32, cbf, kr32, krbf = _mla_project(xs, pos_s, *proj)
            c_new = c32.reshape(bs, ts, KV_LORA)
            kr_new = kr32.reshape(bs, ts, QK_ROPE)
            o_lat = _mla_decode_attn(ql, qr, cache_mla_latent[j], cache_mla_krope[j], c_new, kr_new)
            xs = _mla_finish(xs, o_lat, w_uv_t, w_mla_o[j])
            lat_s.append(c_new)
            kr_s.append(kr_new)

        mk = _norm_matmul(mem2d, w_xa_k, norm_mem[l], layer=l, name="mem_k")
        mv = _norm_matmul(mem2d, w_xa_v, norm_mem[l], layer=l, name="mem_v")
        hd = d // XA_HEADS
        mem_k_p.append(mk.reshape(bp, n_mem, XA_HEADS, hd))
        mem_v_p.append(mv.reshape(bp, n_mem, XA_HEADS, hd))
        xp = _xattn_layer(xp, bp, tp, mk.reshape(bp, n_mem, d), mv.reshape(bp, n_mem, d),
                          norm_xa[l], w_xa_q, w_xa_o, l)
        xs = _xattn_layer(xs, bs, ts, cache_mem_k[l].reshape(bs, n_mem, d),
                          cache_mem_v[l].reshape(bs, n_mem, d), norm_xa[l], w_xa_q, w_xa_o, l)

        u_tab = peer_u[l].astype(BF16)
        v_tab = peer_v[l].astype(BF16)
        xp = _peer_layer(xp, norm_ffn[l], peer_w_q, l, peer_subkeys[l], u_tab, v_tab)
        xs = _peer_layer(xs, norm_ffn[l], peer_w_q, l, peer_subkeys[l], u_tab, v_tab)

    y_prompt = _final_norm(xp, norm_final).reshape(bp, tp, d)
    y_sample = _final_norm(xs, norm_final).reshape(bs, ts, d)
    return (y_prompt, y_sample,
            jnp.stack(conv_a_p), jnp.stack(conv_b_p), jnp.stack(lat_p), jnp.stack(kr_p),
            jnp.stack(mem_k_p), jnp.stack(mem_v_p),
            jnp.stack(conv_a_s), jnp.stack(conv_b_s), jnp.stack(lat_s), jnp.stack(kr_s))
```

```python
import functools

import jax
import jax.numpy as jnp
from jax import lax
from jax.experimental import pallas as pl
from jax.experimental.pallas import tpu as pltpu

F32 = jnp.float32
BF16 = jnp.bfloat16

EPS = 1e-6
CHUNK = 64
CONV_A_WIDTH = 31
CONV_B_WIDTH = 3
MLA_HEADS = 16
Q_LORA = 512
KV_LORA = 512
QK_NOPE = 128
QK_ROPE = 64
V_DIM = 128
ROPE_THETA = 10000.0
MLA_SCALE = (QK_NOPE + QK_ROPE) ** -0.5
XA_HEADS = 4
PEER_HEADS = 8
N_KEYS = 128
PEER_TOPK = 16

LANES = 128
NEG_BIG = -1e30
MIB = 1024 * 1024
VMEM_LIMIT = 56 * MIB


def _params(n_axes, vmem=VMEM_LIMIT):
    return pltpu.CompilerParams(dimension_semantics=("arbitrary",) * n_axes,
                                vmem_limit_bytes=vmem)


def _dot_nt(a, b):
    return lax.dot_general(a, b, (((1,), (1,)), ((), ())), preferred_element_type=F32)


def _rms(x, g):
    return x * lax.rsqrt(jnp.mean(x * x, axis=-1, keepdims=True) + EPS) * g


def _nm_body(*refs, has_norm, has_res):
    it = iter(refs)
    x_ref = next(it)
    g_ref = next(it) if has_norm else None
    w_ref = next(it)
    r_ref = next(it) if has_res else None
    o_ref = next(it)
    xn_ref = next(it)

    @pl.when(pl.program_id(1) == 0)
    def _():
        x = x_ref[...].astype(F32)
        if has_norm:
            x = _rms(x, g_ref[...])
        xn_ref[...] = x.astype(BF16)

    acc = jnp.dot(xn_ref[...], w_ref[...].astype(BF16), preferred_element_type=F32)
    if has_res:
        acc = acc + r_ref[...]
    o_ref[...] = acc.astype(o_ref.dtype)


def _norm_matmul(x, w, g=None, res=None, *, layer=None, x_col_block=0, out_dtype=F32, name):
    m = x.shape[0]
    k, n = w.shape[-2:]
    tm = min(m, 1024)
    tn = 512 if n % 512 == 0 else n
    has_norm = g is not None
    has_res = res is not None
    in_specs = [pl.BlockSpec((tm, k), lambda i, j: (i, x_col_block))]
    args = [x]
    if has_norm:
        in_specs.append(pl.BlockSpec((1, k), lambda i, j: (0, 0)))
        args.append(g.reshape(1, k))
    if layer is None:
        in_specs.append(pl.BlockSpec((k, tn), lambda i, j: (0, j)))
    else:
        in_specs.append(pl.BlockSpec((None, k, tn), lambda i, j: (layer, 0, j)))
    args.append(w)
    if has_res:
        in_specs.append(pl.BlockSpec((tm, tn), lambda i, j: (i, j)))
        args.append(res)
    return pl.pallas_call(
        functools.partial(_nm_body, has_norm=has_norm, has_res=has_res),
        grid=(m // tm, n // tn),
        in_specs=in_specs,
        out_specs=pl.BlockSpec((tm, tn), lambda i, j: (i, j)),
        out_shape=jax.ShapeDtypeStruct((m, n), out_dtype),
        scratch_shapes=[pltpu.VMEM((tm, k), BF16)],
        compiler_params=_params(2),
        name=name,
    )(*args)


A_HIST = 32
B_HIST = 8


def _conv_body(av_ref, ag_ref, bg_ref, cg_ref, v_ref, ctxa_ref, ctxb_ref, caw_ref, cab_ref,
               lng_ref, lnb_ref, cbw_ref, y_ref, ta_ref, tb_ref, sa_ref, sb_ref, *, tt, d_a):
    t = pl.program_id(1)

    @pl.when(t == 0)
    def _():
        sa_ref[0:A_HIST, :] = ctxa_ref[...]
        sb_ref[0:B_HIST, :] = ctxb_ref[...]

    @pl.when(t > 0)
    def _():
        sa_ref[0:A_HIST, :] = sa_ref[tt:tt + A_HIST, :]
        sb_ref[0:B_HIST, :] = sb_ref[tt:tt + B_HIST, :]

    sa_ref[A_HIST:A_HIST + tt, :] = av_ref[...] * jax.nn.sigmoid(ag_ref[...])
    sb_ref[B_HIST:B_HIST + tt, :] = cg_ref[...] * v_ref[...]

    off_a = A_HIST - (CONV_A_WIDTH - 1)
    acc = jnp.broadcast_to(cab_ref[...], (tt, d_a))
    for k in range(CONV_A_WIDTH):
        acc = acc + caw_ref[k:k + 1, :] * sa_ref[off_a + k:off_a + k + tt, :]
    mu = jnp.mean(acc, axis=-1, keepdims=True)
    xc = acc - mu
    var = jnp.mean(xc * xc, axis=-1, keepdims=True)
    ln = xc * lax.rsqrt(var + EPS) * lng_ref[...] + lnb_ref[...]
    y_ref[:, 0:d_a] = (ln * jax.nn.sigmoid(ln)).astype(y_ref.dtype)

    off_b = B_HIST - (CONV_B_WIDTH - 1)
    accb = cbw_ref[0:1, :] * sb_ref[off_b:off_b + tt, :]
    for k in range(1, CONV_B_WIDTH):
        accb = accb + cbw_ref[k:k + 1, :] * sb_ref[off_b + k:off_b + k + tt, :]
    y_ref[:, d_a:] = (bg_ref[...] * accb).astype(y_ref.dtype)

    ta_ref[...] = sa_ref[tt:tt + A_HIST, :]
    tb_ref[...] = sb_ref[tt:tt + B_HIST, :]


def _conv_mix(z, ctx_a, ctx_b, ca_w, ca_b, ln_g, ln_b, cb_w):
    nb, t, _ = z.shape
    d_a = ca_w.shape[-1]
    tt = min(t, 256)
    ctx_a = jnp.pad(ctx_a, ((0, 0), (A_HIST - ctx_a.shape[1], 0), (0, 0)))
    ctx_b = jnp.pad(ctx_b, ((0, 0), (B_HIST - ctx_b.shape[1], 0), (0, 0)))
    caw = jnp.pad(ca_w, ((0, A_HIST - CONV_A_WIDTH), (0, 0)))
    cbw = jnp.pad(cb_w, ((0, B_HIST - CONV_B_WIDTH), (0, 0)))

    def zspec(col):
        return pl.BlockSpec((None, tt, d_a), lambda b, i: (b, i, col))

    def vec(rows):
        return pl.BlockSpec((rows, d_a), lambda b, i: (0, 0))

    y, ta, tb = pl.pallas_call(
        functools.partial(_conv_body, tt=tt, d_a=d_a),
        grid=(nb, t // tt),
        in_specs=[zspec(0), zspec(1), zspec(2), zspec(3), zspec(4),
                  pl.BlockSpec((None, A_HIST, d_a), lambda b, i: (b, 0, 0)),
                  pl.BlockSpec((None, B_HIST, d_a), lambda b, i: (b, 0, 0)),
                  vec(A_HIST), vec(1), vec(1), vec(1), vec(B_HIST)],
        out_specs=[pl.BlockSpec((None, tt, 2 * d_a), lambda b, i: (b, i, 0)),
                   pl.BlockSpec((None, A_HIST, d_a), lambda b, i: (b, 0, 0)),
                   pl.BlockSpec((None, B_HIST, d_a), lambda b, i: (b, 0, 0))],
        out_shape=[jax.ShapeDtypeStruct((nb, t, 2 * d_a), BF16),
                   jax.ShapeDtypeStruct((nb, A_HIST, d_a), F32),
                   jax.ShapeDtypeStruct((nb, B_HIST, d_a), F32)],
        scratch_shapes=[pltpu.VMEM((tt + A_HIST, d_a), F32),
                        pltpu.VMEM((tt + B_HIST, d_a), F32)],
        compiler_params=_params(2),
        name="conv_mix",
    )(z, z, z, z, z, ctx_a, ctx_b, caw, ca_b.reshape(1, d_a), ln_g.reshape(1, d_a),
      ln_b.reshape(1, d_a), cbw)
    return (y, ta[:, A_HIST - (CONV_A_WIDTH - 1):], tb[:, B_HIST - (CONV_B_WIDTH - 1):])


def _qlat_body(q_ref, w_ref, o_ref, *, tm):
    for h in range(MLA_HEADS):
        a = q_ref[:, h * QK_NOPE:(h + 1) * QK_NOPE].astype(BF16)
        r = jnp.dot(a, w_ref[h].astype(BF16), preferred_element_type=F32).astype(BF16)
        for j in range(tm // CHUNK):
            o_ref[j, h] = r[j * CHUNK:(j + 1) * CHUNK]


def _mla_qlat(q, w_uk_t):
    m = q.shape[0]
    tm = min(m, 512)
    return pl.pallas_call(
        functools.partial(_qlat_body, tm=tm),
        grid=(m // tm,),
        in_specs=[pl.BlockSpec((tm, MLA_HEADS * QK_NOPE), lambda i: (i, 0)),
                  pl.BlockSpec((MLA_HEADS, QK_NOPE, KV_LORA), lambda i: (0, 0, 0))],
        out_specs=pl.BlockSpec((tm // CHUNK, MLA_HEADS, CHUNK, KV_LORA), lambda i: (i, 0, 0, 0)),
        out_shape=jax.ShapeDtypeStruct((m // CHUNK, MLA_HEADS, CHUNK, KV_LORA), BF16),
        compiler_params=_params(1),
        name="mla_qlat",
    )(q, w_uk_t)


def _mla_post_body(z_ref, qr_ref, cos_ref, sin_ref, g_ref, c32_ref, cbf_ref, kr32_ref, krbf_ref,
                   qro_ref, *, tm):
    z = z_ref[...]
    c = _rms(z[:, Q_LORA:Q_LORA + KV_LORA], g_ref[...])
    c32_ref[...] = c
    cbf_ref[...] = c.astype(BF16)

    cos = cos_ref[...]
    sin = sin_ref[...]
    kr = z[:, Q_LORA + KV_LORA:]
    half = QK_ROPE // 2
    kr_sw = jnp.concatenate([kr[:, half:], kr[:, :half]], axis=1)
    kr_rot = kr * cos[:, :QK_ROPE] + kr_sw * sin[:, :QK_ROPE]
    kr32_ref[...] = kr_rot
    krbf_ref[...] = kr_rot.astype(BF16)

    q = qr_ref[...]
    width = q.shape[1]
    lane = lax.broadcasted_iota(jnp.int32, q.shape, 1)
    first_half = (lane % QK_ROPE) < half
    q_sw = jnp.where(first_half, pltpu.roll(q, width - half, axis=1), pltpu.roll(q, half, axis=1))
    reps = width // LANES
    cos_t = jnp.concatenate([cos] * reps, axis=1)
    sin_t = jnp.concatenate([sin] * reps, axis=1)
    q_rot = q * cos_t + q_sw * sin_t
    for h in range(MLA_HEADS):
        piece = q_rot[:, h * QK_ROPE:(h + 1) * QK_ROPE].astype(BF16)
        for j in range(tm // CHUNK):
            qro_ref[j, h] = piece[j * CHUNK:(j + 1) * CHUNK]


def _mla_post(z, q, cos, sin, kv_norm):
    m = z.shape[0]
    tm = min(m, 512)
    rope_w = MLA_HEADS * QK_ROPE
    rope_block = (MLA_HEADS * QK_NOPE) // rope_w
    return pl.pallas_call(
        functools.partial(_mla_post_body, tm=tm),
        grid=(m // tm,),
        in_specs=[pl.BlockSpec((tm, z.shape[1]), lambda i: (i, 0)),
                  pl.BlockSpec((tm, rope_w), lambda i: (i, rope_block)),
                  pl.BlockSpec((tm, LANES), lambda i: (i, 0)),
                  pl.BlockSpec((tm, LANES), lambda i: (i, 0)),
                  pl.BlockSpec((1, KV_LORA), lambda i: (0, 0))],
        out_specs=[pl.BlockSpec((tm, KV_LORA), lambda i: (i, 0)),
                   pl.BlockSpec((tm, KV_LORA), lambda i: (i, 0)),
                   pl.BlockSpec((tm, QK_ROPE), lambda i: (i, 0)),
                   pl.BlockSpec((tm, QK_ROPE), lambda i: (i, 0)),
                   pl.BlockSpec((tm // CHUNK, MLA_HEADS, CHUNK, QK_ROPE), lambda i: (i, 0, 0, 0))],
        out_shape=[jax.ShapeDtypeStruct((m, KV_LORA), F32),
                   jax.ShapeDtypeStruct((m, KV_LORA), BF16),
                   jax.ShapeDtypeStruct((m, QK_ROPE), F32),
                   jax.ShapeDtypeStruct((m, QK_ROPE), BF16),
                   jax.ShapeDtypeStruct((m // CHUNK, MLA_HEADS, CHUNK, QK_ROPE), BF16)],
        compiler_params=_params(1),
        name="mla_post",
    )(z, q, cos, sin, kv_norm.reshape(1, KV_LORA))


def _rope_tables(pos):
    inv = 1.0 / (ROPE_THETA ** (jnp.arange(0, QK_ROPE, 2, dtype=F32) / QK_ROPE))
    ang = pos.astype(F32)[:, None] * inv[None, :]
    cos, sin = jnp.cos(ang), jnp.sin(ang)
    return (jnp.concatenate([cos, cos, cos, cos], axis=1),
            jnp.concatenate([-sin, sin, -sin, sin], axis=1))


MLA_KB = 512


def _mla_attn_body(ql_ref, qr_ref, c_ref, kr_ref, o_ref, s_ref, m_ref, l_ref, acc_ref):
    ci = pl.program_id(1)
    rows = MLA_HEADS * CHUNK
    ql = ql_ref[...].reshape(rows, KV_LORA)
    qr = qr_ref[...].reshape(rows, QK_ROPE)
    n_valid = (ci + 1) * CHUNK
    n_full = n_valid // MLA_KB
    tiles = MLA_KB // LANES
    m_ref[...] = jnp.full((rows, LANES), NEG_BIG, F32)

    def scores(kb, n_keep):
        k0 = pl.multiple_of(kb * MLA_KB, MLA_KB)
        s = (_dot_nt(ql, c_ref[pl.ds(k0, MLA_KB), :])
             + _dot_nt(qr, kr_ref[pl.ds(k0, MLA_KB), :])) * MLA_SCALE
        if n_keep is not None:
            col = lax.broadcasted_iota(jnp.int32, s.shape, 1)
            s = jnp.where(col < n_keep, s, NEG_BIG)
        s_ref[kb] = s
        mp = m_ref[...]
        for t in range(tiles):
            mp = jnp.maximum(mp, s[:, t * LANES:(t + 1) * LANES])
        m_ref[...] = mp

    def pass1(kb, carry):
        scores(kb, None)
        return carry

    lax.fori_loop(0, n_full, pass1, 0)
    rem = n_valid - n_full * MLA_KB

    @pl.when(rem > 0)
    def _():
        scores(n_full, rem)

    m = jnp.max(m_ref[...], axis=-1, keepdims=True)
    m_ref[...] = jnp.broadcast_to(m, (rows, LANES))
    l_ref[...] = jnp.zeros((rows, LANES), F32)
    acc_ref[...] = jnp.zeros((rows, KV_LORA), F32)

    def pass2(kb, carry):
        k0 = pl.multiple_of(kb * MLA_KB, MLA_KB)
        m_rep = m_ref[...]
        p = jnp.exp(s_ref[kb] - jnp.concatenate([m_rep] * tiles, axis=1))
        lp = l_ref[...]
        for t in range(tiles):
            lp = lp + p[:, t * LANES:(t + 1) * LANES]
        l_ref[...] = lp
        acc_ref[...] += jnp.dot(p.astype(BF16), c_ref[pl.ds(k0, MLA_KB), :],
                                preferred_element_type=F32)
        return carry

    lax.fori_loop(0, (n_valid + MLA_KB - 1) // MLA_KB, pass2, 0)
    l = jnp.sum(l_ref[...], axis=-1, keepdims=True)
    o_ref[...] = (acc_ref[...] / l).astype(o_ref.dtype).reshape(MLA_HEADS, CHUNK, KV_LORA)


def _mla_prompt_attn(qlat, qrope, cbf, krbf):
    nb, t, _ = cbf.shape
    nc = t // CHUNK
    rows = MLA_HEADS * CHUNK
    return pl.pallas_call(
        _mla_attn_body,
        grid=(nb, nc),
        in_specs=[pl.BlockSpec((None, MLA_HEADS, CHUNK, KV_LORA), lambda b, i: (b * nc + i, 0, 0, 0)),
                  pl.BlockSpec((None, MLA_HEADS, CHUNK, QK_ROPE), lambda b, i: (b * nc + i, 0, 0, 0)),
                  pl.BlockSpec((None, t, KV_LORA), lambda b, i: (b, 0, 0)),
                  pl.BlockSpec((None, t, QK_ROPE), lambda b, i: (b, 0, 0))],
        out_specs=pl.BlockSpec((None, MLA_HEADS, CHUNK, KV_LORA), lambda b, i: (b * nc + i, 0, 0, 0)),
        out_shape=jax.ShapeDtypeStruct(qlat.shape, BF16),
        scratch_shapes=[pltpu.VMEM((t // MLA_KB, rows, MLA_KB), F32),
                        pltpu.VMEM((rows, LANES), F32), pltpu.VMEM((rows, LANES), F32),
                        pltpu.VMEM((rows, KV_LORA), F32)],
        compiler_params=_params(2),
        name="mla_prompt_attn",
    )(qlat, qrope, cbf, krbf)


def _mla_dec_body(ql_ref, qr_ref, cc_ref, ckr_ref, cn_ref, krn_ref, o_ref, *, n_new):
    rows = MLA_HEADS * n_new
    ql = ql_ref[...].reshape(rows, KV_LORA)
    qr = qr_ref[...].reshape(rows, QK_ROPE)
    cc = cc_ref[...].astype(BF16)
    cn = cn_ref[...].astype(BF16)
    s1 = (_dot_nt(ql, cc) + _dot_nt(qr, ckr_ref[...].astype(BF16))) * MLA_SCALE
    s2 = (_dot_nt(ql, cn) + _dot_nt(qr, krn_ref[...].astype(BF16))) * MLA_SCALE
    col = lax.broadcasted_iota(jnp.int32, s2.shape, 1)
    s2 = jnp.where(col < n_new, s2, NEG_BIG)
    m = jnp.maximum(jnp.max(s1, axis=-1, keepdims=True), jnp.max(s2, axis=-1, keepdims=True))
    p1 = jnp.exp(s1 - m)
    p2 = jnp.exp(s2 - m)
    l = jnp.sum(p1, axis=-1, keepdims=True) + jnp.sum(p2, axis=-1, keepdims=True)
    o = (jnp.dot(p1.astype(BF16), cc, preferred_element_type=F32)
         + jnp.dot(p2.astype(BF16), cn, preferred_element_type=F32))
    o_ref[...] = (o / l).astype(o_ref.dtype).reshape(MLA_HEADS, n_new, KV_LORA)


def _mla_decode_attn(qlat, qrope, cache_c, cache_kr, c_new, kr_new):
    nb, past, _ = cache_c.shape
    n_new = c_new.shape[1]
    per_chunk = CHUNK // n_new
    c_pad = jnp.pad(c_new, ((0, 0), (0, LANES - n_new), (0, 0)))
    kr_pad = jnp.pad(kr_new, ((0, 0), (0, LANES - n_new), (0, 0)))

    def qspec(width):
        return pl.BlockSpec((None, MLA_HEADS, n_new, width),
                            lambda b: (b // per_chunk, 0, b % per_chunk, 0))

    return pl.pallas_call(
        functools.partial(_mla_dec_body, n_new=n_new),
        grid=(nb,),
        in_specs=[qspec(KV_LORA), qspec(QK_ROPE),
                  pl.BlockSpec((None, past, KV_LORA), lambda b: (b, 0, 0)),
                  pl.BlockSpec((None, past, QK_ROPE), lambda b: (b, 0, 0)),
                  pl.BlockSpec((None, LANES, KV_LORA), lambda b: (b, 0, 0)),
                  pl.BlockSpec((None, LANES, QK_ROPE), lambda b: (b, 0, 0))],
        out_specs=qspec(KV_LORA),
        out_shape=jax.ShapeDtypeStruct(qlat.shape, BF16),
        compiler_params=_params(1),
        name="mla_decode_attn",
    )(qlat, qrope, cache_c, cache_kr, c_pad, kr_pad)


def _uv_body(o_ref, w_ref, out_ref, *, tm):
    for h in range(MLA_HEADS):
        a = jnp.concatenate([o_ref[j, h] for j in range(tm // CHUNK)], axis=0)
        out_ref[:, h * V_DIM:(h + 1) * V_DIM] = jnp.dot(
            a, w_ref[h].astype(BF16), preferred_element_type=F32).astype(out_ref.dtype)


def _mla_uv(o_lat, w_uv_t):
    m = o_lat.shape[0] * CHUNK
    tm = min(m, 512)
    return pl.pallas_call(
        functools.partial(_uv_body, tm=tm),
        grid=(m // tm,),
        in_specs=[pl.BlockSpec((tm // CHUNK, MLA_HEADS, CHUNK, KV_LORA), lambda i: (i, 0, 0, 0)),
                  pl.BlockSpec((MLA_HEADS, KV_LORA, V_DIM), lambda i: (0, 0, 0))],
        out_specs=pl.BlockSpec((tm, MLA_HEADS * V_DIM), lambda i: (i, 0)),
        out_shape=jax.ShapeDtypeStruct((m, MLA_HEADS * V_DIM), BF16),
        compiler_params=_params(1),
        name="mla_uv",
    )(o_lat, w_uv_t)


def _xattn_body(q_ref, k_ref, v_ref, o_ref, *, heads, hd):
    scale = hd ** -0.5
    for h in range(heads):
        sl = slice(h * hd, (h + 1) * hd)
        q = q_ref[:, sl].astype(BF16)
        k = k_ref[:, sl].astype(BF16)
        v = v_ref[:, sl].astype(BF16)
        s = _dot_nt(q, k) * scale
        e = jnp.exp(s - jnp.max(s, axis=-1, keepdims=True))
        p = e / jnp.sum(e, axis=-1, keepdims=True)
        o_ref[:, sl] = jnp.dot(p.astype(BF16), v, preferred_element_type=F32).astype(o_ref.dtype)


def _xattn(q, k, v):
    nb, t, d = q.shape
    n_mem = k.shape[1]
    tt = min(t, 512)
    return pl.pallas_call(
        functools.partial(_xattn_body, heads=XA_HEADS, hd=d // XA_HEADS),
        grid=(nb, t // tt),
        in_specs=[pl.BlockSpec((None, tt, d), lambda b, i: (b, i, 0)),
                  pl.BlockSpec((None, n_mem, d), lambda b, i: (b, 0, 0)),
                  pl.BlockSpec((None, n_mem, d), lambda b, i: (b, 0, 0))],
        out_specs=pl.BlockSpec((None, tt, d), lambda b, i: (b, i, 0)),
        out_shape=jax.ShapeDtypeStruct((nb, t, d), BF16),
        compiler_params=_params(2),
        name="xattn",
    )(q, k, v)


ROUTE_TN = LANES
STAT_ROWS = 8


def _sorted_top_pair(w0, w1, n, rows_out):
    rows0, rows1 = [], []
    for r in range(n):
        m0 = jnp.max(w0, axis=0, keepdims=True)
        m1 = jnp.max(w1, axis=0, keepdims=True)
        rows0.append(m0)
        rows1.append(m1)
        if r + 1 < n:
            w0 = jnp.where(w0 == m0, -jnp.inf, w0)
            w1 = jnp.where(w1 == m1, -jnp.inf, w1)
    pad = [jnp.full((rows_out - n, w0.shape[1]), -jnp.inf, F32)]
    return jnp.concatenate(rows0 + pad, axis=0), jnp.concatenate(rows1 + pad, axis=0)


ROUTE_N = PEER_TOPK + 1
ROUTE_ROWS = 24


def _pair_sum_candidates(a, b):
    g = 8
    pieces = [a[0:1] + b]
    pieces += [a[r:r + 1] + b[0:g] for r in range(1, g)]
    pieces.append(a[g:ROUTE_ROWS] + b[0:1])
    return jnp.concatenate(pieces, axis=0)


def _route_body(q_ref, sk_ref, s0_ref, s1_ref, st_ref):
    sk0 = sk_ref[0].astype(BF16)
    sk1 = sk_ref[1].astype(BF16)
    half = sk_ref.shape[2]

    def head_lists(h):
        c0 = pl.multiple_of(h * 2 * half, LANES)
        c1 = pl.multiple_of(h * 2 * half + half, LANES)
        s0 = _dot_nt(sk0, q_ref[:, pl.ds(c0, half)].astype(BF16))
        s1 = _dot_nt(sk1, q_ref[:, pl.ds(c1, half)].astype(BF16))
        s0_ref[h] = s0
        s1_ref[h] = s1
        return _sorted_top_pair(s0, s1, ROUTE_N, ROUTE_ROWS)

    def write_stats(h, a, b, cand, top):
        tau = 0.5 * (top[PEER_TOPK - 1:PEER_TOPK] + top[PEER_TOPK:PEER_TOPK + 1])
        z = jnp.sum(jnp.where(cand >= tau, jnp.exp(cand - top[0:1]), 0.0), axis=0, keepdims=True)
        st_ref[h] = jnp.concatenate(
            [tau, a[0:1], b[0:1], 1.0 / z, jnp.zeros((STAT_ROWS - 4, ROUTE_TN), F32)], axis=0)

    def head_pair(hp, carry):
        h0 = hp * 2
        a0, b0 = head_lists(h0)
        a1, b1 = head_lists(h0 + 1)
        cand0 = _pair_sum_candidates(a0, b0)
        cand1 = _pair_sum_candidates(a1, b1)
        top0, top1 = _sorted_top_pair(cand0, cand1, ROUTE_N, ROUTE_ROWS)
        write_stats(h0, a0, b0, cand0, top0)
        write_stats(h0 + 1, a1, b1, cand1, top1)
        return carry

    lax.fori_loop(0, PEER_HEADS // 2, head_pair, 0)


def _peer_route(q, subkeys):
    n = q.shape[0]
    half = subkeys.shape[2]
    spec_s = pl.BlockSpec((PEER_HEADS, N_KEYS, ROUTE_TN), lambda i: (0, 0, i))
    return pl.pallas_call(
        _route_body,
        grid=(n // ROUTE_TN,),
        in_specs=[pl.BlockSpec((ROUTE_TN, q.shape[1]), lambda i: (i, 0)),
                  pl.BlockSpec((2, N_KEYS, half), lambda i: (0, 0, 0))],
        out_specs=[spec_s, spec_s,
                   pl.BlockSpec((PEER_HEADS, STAT_ROWS, ROUTE_TN), lambda i: (0, 0, i))],
        out_shape=[jax.ShapeDtypeStruct((PEER_HEADS, N_KEYS, n), F32),
                   jax.ShapeDtypeStruct((PEER_HEADS, N_KEYS, n), F32),
                   jax.ShapeDtypeStruct((PEER_HEADS, STAT_ROWS, n), F32)],
        compiler_params=_params(1),
        name="peer_route",
    )(q, subkeys)


PEER_EB = 1024


def _gelu_tanh(x):
    return 0.5 * x * (1.0 + jnp.tanh(0.7978845608028654 * (x + 0.044715 * (x * x * x))))


def _peer_body(x_ref, g_ref, u_ref, v_ref, s0_ref, s1_ref, st_ref, *rest, tn, final_norm):
    fg_ref = rest[0] if final_norm else None
    o_ref, xn_ref, e1_ref, w_ref = rest[-4:]
    @pl.when(pl.program_id(1) == 0)
    def _():
        xn_ref[...] = _rms(x_ref[...], g_ref[...]).astype(BF16)
        for h in range(PEER_HEADS):
            e1_ref[h] = jnp.exp(s1_ref[h] - st_ref[2, h:h + 1, :])
        o_ref[...] = x_ref[...]

    act = _dot_nt(u_ref[...], xn_ref[...])
    for c in range(tn // LANES):
        lanes = slice(c * LANES, (c + 1) * LANES)
        for ii in range(PEER_EB // N_KEYS):
            s0rows = s0_ref[ii, :, lanes]
            thr_all = st_ref[0, :, lanes] - s0rows
            coef_all = jnp.exp(s0rows - st_ref[1, :, lanes]) * st_ref[3, :, lanes]
            g = None
            for h in range(PEER_HEADS):
                gh = jnp.where(s1_ref[h, :, lanes] >= thr_all[h:h + 1],
                               e1_ref[h, :, lanes] * coef_all[h:h + 1], 0.0)
                g = gh if g is None else g + gh
            a = act[ii * N_KEYS:(ii + 1) * N_KEYS, lanes]
            w_ref[ii * N_KEYS:(ii + 1) * N_KEYS, lanes] = (_gelu_tanh(a) * g).astype(BF16)
    o_ref[...] += lax.dot_general(w_ref[...], v_ref[...], (((0,), (0,)), ((), ())),
                                  preferred_element_type=F32)

    if fg_ref is not None:
        @pl.when(pl.program_id(1) == pl.num_programs(1) - 1)
        def _():
            o_ref[...] = _rms(o_ref[...], fg_ref[...])


def _peer_dense(x, g, u_tab, v_tab, s0t, s1t, stats, final_g=None):
    n, d = x.shape
    n_exp = u_tab.shape[0]
    tn = min(n, 512)
    groups = PEER_EB // N_KEYS
    s0_by_key = jnp.transpose(s0t, (1, 0, 2))
    stats_by_kind = jnp.transpose(stats, (1, 0, 2))
    extra_specs = [] if final_g is None else [pl.BlockSpec((1, d), lambda i, e: (0, 0))]
    extra_args = [] if final_g is None else [final_g.reshape(1, d)]
    return pl.pallas_call(
        functools.partial(_peer_body, tn=tn, final_norm=final_g is not None),
        grid=(n // tn, n_exp // PEER_EB),
        in_specs=[pl.BlockSpec((tn, d), lambda i, e: (i, 0)),
                  pl.BlockSpec((1, d), lambda i, e: (0, 0)),
                  pl.BlockSpec((PEER_EB, d), lambda i, e: (e, 0)),
                  pl.BlockSpec((PEER_EB, d), lambda i, e: (e, 0)),
                  pl.BlockSpec((groups, PEER_HEADS, tn), lambda i, e: (e, 0, i)),
                  pl.BlockSpec((PEER_HEADS, N_KEYS, tn), lambda i, e: (0, 0, i)),
                  pl.BlockSpec((STAT_ROWS, PEER_HEADS, tn), lambda i, e: (0, 0, i))] + extra_specs,
        out_specs=pl.BlockSpec((tn, d), lambda i, e: (i, 0)),
        out_shape=jax.ShapeDtypeStruct((n, d), F32),
        scratch_shapes=[pltpu.VMEM((tn, d), BF16),
                        pltpu.VMEM((PEER_HEADS, N_KEYS, tn), F32),
                        pltpu.VMEM((PEER_EB, tn), BF16)],
        compiler_params=_params(2),
        name="peer_dense",
    )(x, g.reshape(1, d), u_tab, v_tab, s0_by_key, s1t, stats_by_kind, *extra_args)


def _conv_layer(x, nb, t, ctx_a, ctx_b, g, w_in, ca_w, ca_b, ln_g, ln_b, cb_w, w_out):
    z = _norm_matmul(x, w_in, g, name="conv_in")
    y, new_a, new_b = _conv_mix(z.reshape(nb, t, -1), ctx_a, ctx_b, ca_w, ca_b, ln_g, ln_b, cb_w)
    x = _norm_matmul(y.reshape(nb * t, -1), w_out, res=x, name="conv_out")
    return x, new_a, new_b


def _mla_project(x, pos, g, w_down, q_norm, kv_norm, w_uq_perm, w_uk_t):
    z = _norm_matmul(x, w_down, g, name="mla_down")
    q = _norm_matmul(z, w_uq_perm, q_norm, name="mla_uq")
    qlat = _mla_qlat(q, w_uk_t)
    cos, sin = _rope_tables(pos)
    c32, cbf, kr32, krbf, qrope = _mla_post(z, q, cos, sin, kv_norm)
    return qlat, qrope, c32, cbf, kr32, krbf


def _mla_finish(x, o_lat, w_uv_t, w_o):
    return _norm_matmul(_mla_uv(o_lat, w_uv_t), w_o, res=x, name="mla_o")


def _xattn_layer(x, nb, t, k, v, g, wq, wo, layer):
    d = x.shape[1]
    q = _norm_matmul(x, wq, g, layer=layer, name="xa_q")
    o = _xattn(q.reshape(nb, t, d), k, v)
    return _norm_matmul(o.reshape(nb * t, d), wo, res=x, layer=layer, name="xa_o")


def _peer_layer(x, g, w_q, layer, subkeys, u_tab, v_tab, final_g=None):
    q = _norm_matmul(x, w_q, g, layer=layer, name="peer_q")
    s0t, s1t, stats = _peer_route(q, subkeys)
    return _peer_dense(x, g, u_tab, v_tab, s0t, s1t, stats, final_g)


def kernel(x_prompt, x_sample, cache_conv_a, cache_conv_b, cache_mla_latent, cache_mla_krope, cache_mem_k, cache_mem_v, mem_prompt, norm_mix, w_conv_in, conv_a_w, conv_a_b, ln_a_g, ln_a_b, conv_b_w, w_conv_out, w_mla_down, mla_q_norm, mla_kv_norm, w_mla_uq, w_mla_uk, w_mla_uv, w_mla_o, norm_xa, norm_mem, w_xa_q, w_xa_k, w_xa_v, w_xa_o, norm_ffn, peer_w_q, peer_subkeys, peer_u, peer_v, norm_final):
    bp, tp, d = x_prompt.shape
    bs, ts, _ = x_sample.shape
    depth = norm_mix.shape[0]
    past = cache_mla_latent.shape[2]
    n_mem = mem_prompt.shape[1]
    xp = x_prompt.reshape(bp * tp, d)
    xs = x_sample.reshape(bs * ts, d)
    pos_p = jnp.tile(jnp.arange(tp, dtype=jnp.int32), bp)
    pos_s = jnp.tile(past + jnp.arange(ts, dtype=jnp.int32), bs)
    mem2d = mem_prompt.reshape(bp * n_mem, d)

    conv_a_p, conv_b_p, lat_p, kr_p, mem_k_p, mem_v_p = [], [], [], [], [], []
    conv_a_s, conv_b_s, lat_s, kr_s = [], [], [], []
    for l in range(depth):
        j = l // 2
        if l % 2 == 0:
            d_a = conv_a_w.shape[-1]
            zero_a = jnp.zeros((bp, CONV_A_WIDTH - 1, d_a), F32)
            zero_b = jnp.zeros((bp, CONV_B_WIDTH - 1, conv_b_w.shape[-1]), F32)
            wts = (norm_mix[l], w_conv_in[j], conv_a_w[j], conv_a_b[j], ln_a_g[j], ln_a_b[j],
                   conv_b_w[j], w_conv_out[j])
            xp, sa, sb = _conv_layer(xp, bp, tp, zero_a, zero_b, *wts)
            conv_a_p.append(sa)
            conv_b_p.append(sb)
            xs, sa, sb = _conv_layer(xs, bs, ts, cache_conv_a[j], cache_conv_b[j], *wts)
            conv_a_s.append(sa)
            conv_b_s.append(sb)
        else:
            uq = w_mla_uq[j].reshape(Q_LORA, MLA_HEADS, QK_NOPE + QK_ROPE)
            w_uq_perm = jnp.concatenate(
                [uq[:, :, :QK_NOPE].reshape(Q_LORA, MLA_HEADS * QK_NOPE),
                 uq[:, :, QK_NOPE:].reshape(Q_LORA, MLA_HEADS * QK_ROPE)], axis=1)
            w_uk_t = jnp.transpose(w_mla_uk[j], (1, 2, 0))
            w_uv_t = jnp.transpose(w_mla_uv[j], (1, 0, 2))
            proj = (norm_mix[l], w_mla_down[j], mla_q_norm[j], mla_kv_norm[j], w_uq_perm, w_uk_t)

            ql, qr, c32, cbf, kr32, krbf = _mla_project(xp, pos_p, *proj)
            o_lat = _mla_prompt_attn(ql, qr, cbf.reshape(bp, tp, KV_LORA),
                                     krbf.reshape(bp, tp, QK_ROPE))
            xp = _mla_finish(xp, o_lat, w_uv_t, w_mla_o[j])
            lat_p.append(c32.reshape(bp, tp, KV_LORA))
            kr_p.append(kr32.reshape(bp, tp, QK_ROPE))

            ql, qr, c32, cbf, kr32, krbf = _mla_project(xs, pos_s, *proj)
            c_new = c32.reshape(bs, ts, KV_LORA)
            kr_new = kr32.reshape(bs, ts, QK_ROPE)
            o_lat = _mla_decode_attn(ql, qr, cache_mla_latent[j], cache_mla_krope[j], c_new, kr_new)
            xs = _mla_finish(xs, o_lat, w_uv_t, w_mla_o[j])
            lat_s.append(c_new)
            kr_s.append(kr_new)

        mk = _norm_matmul(mem2d, w_xa_k, norm_mem[l], layer=l, name="mem_k")
        mv = _norm_matmul(mem2d, w_xa_v, norm_mem[l], layer=l, name="mem_v")
        hd = d // XA_HEADS
        mem_k_p.append(mk.reshape(bp, n_mem, XA_HEADS, hd))
        mem_v_p.append(mv.reshape(bp, n_mem, XA_HEADS, hd))
        xp = _xattn_layer(xp, bp, tp, mk.reshape(bp, n_mem, d), mv.reshape(bp, n_mem, d),
                          norm_xa[l], w_xa_q, w_xa_o, l)
        xs = _xattn_layer(xs, bs, ts, cache_mem_k[l].reshape(bs, n_mem, d),
                          cache_mem_v[l].reshape(bs, n_mem, d), norm_xa[l], w_xa_q, w_xa_o, l)

        u_tab = peer_u[l].astype(BF16)
        v_tab = peer_v[l].astype(BF16)
        final_g = norm_final if l == depth - 1 else None
        xp = _peer_layer(xp, norm_ffn[l], peer_w_q, l, peer_subkeys[l], u_tab, v_tab, final_g)
        xs = _peer_layer(xs, norm_ffn[l], peer_w_q, l, peer_subkeys[l], u_tab, v_tab, final_g)

    y_prompt = xp.reshape(bp, tp, d)
    y_sample = xs.reshape(bs, ts, d)
    return (y_prompt, y_sample,
            jnp.stack(conv_a_p), jnp.stack(conv_b_p), jnp.stack(lat_p), jnp.stack(kr_p),
            jnp.stack(mem_k_p), jnp.stack(mem_v_p),
            jnp.stack(conv_a_s), jnp.stack(conv_b_s), jnp.stack(lat_s), jnp.stack(kr_s))
```

```python
import functools

import jax
import jax.numpy as jnp
from jax import lax
from jax.experimental import pallas as pl
from jax.experimental.pallas import tpu as pltpu

F32 = jnp.float32
BF16 = jnp.bfloat16

EPS = 1e-6
CHUNK = 64
CONV_A_WIDTH = 31
CONV_B_WIDTH = 3
MLA_HEADS = 16
Q_LORA = 512
KV_LORA = 512
QK_NOPE = 128
QK_ROPE = 64
V_DIM = 128
ROPE_THETA = 10000.0
MLA_SCALE = (QK_NOPE + QK_ROPE) ** -0.5
XA_HEADS = 4
PEER_HEADS = 8
N_KEYS = 128
PEER_TOPK = 16

LANES = 128
SUBLANES = 8
NEG_BIG = -1e30
MIB = 1024 * 1024
VMEM_LIMIT = 56 * MIB


def _params(n_axes, vmem=VMEM_LIMIT):
    return pltpu.CompilerParams(dimension_semantics=("arbitrary",) * n_axes,
                                vmem_limit_bytes=vmem)


def _dot_nt(a, b):
    return lax.dot_general(a, b, (((1,), (1,)), ((), ())), preferred_element_type=F32)


def _rms(x, g):
    return x * lax.rsqrt(jnp.mean(x * x, axis=-1, keepdims=True) + EPS) * g


def _nm_body(*refs, has_norm, has_res):
    it = iter(refs)
    x_ref = next(it)
    g_ref = next(it) if has_norm else None
    w_ref = next(it)
    r_ref = next(it) if has_res else None
    o_ref = next(it)
    xn_ref = next(it)

    @pl.when(pl.program_id(1) == 0)
    def _():
        x = x_ref[...].astype(F32)
        if has_norm:
            x = _rms(x, g_ref[...])
        xn_ref[...] = x.astype(BF16)

    acc = jnp.dot(xn_ref[...], w_ref[...].astype(BF16), preferred_element_type=F32)
    if has_res:
        acc = acc + r_ref[...]
    o_ref[...] = acc.astype(o_ref.dtype)


def _norm_matmul(x, w, g=None, res=None, *, layer=None, x_col_block=0, out_dtype=F32, name):
    m = x.shape[0]
    k, n = w.shape[-2:]
    tm = min(m, 1024)
    tn = 512 if n % 512 == 0 else n
    has_norm = g is not None
    has_res = res is not None
    in_specs = [pl.BlockSpec((tm, k), lambda i, j: (i, x_col_block))]
    args = [x]
    if has_norm:
        in_specs.append(pl.BlockSpec((1, k), lambda i, j: (0, 0)))
        args.append(g.reshape(1, k))
    if layer is None:
        in_specs.append(pl.BlockSpec((k, tn), lambda i, j: (0, j)))
    else:
        in_specs.append(pl.BlockSpec((None, k, tn), lambda i, j: (layer, 0, j)))
    args.append(w)
    if has_res:
        in_specs.append(pl.BlockSpec((tm, tn), lambda i, j: (i, j)))
        args.append(res)
    return pl.pallas_call(
        functools.partial(_nm_body, has_norm=has_norm, has_res=has_res),
        grid=(m // tm, n // tn),
        in_specs=in_specs,
        out_specs=pl.BlockSpec((tm, tn), lambda i, j: (i, j)),
        out_shape=jax.ShapeDtypeStruct((m, n), out_dtype),
        scratch_shapes=[pltpu.VMEM((tm, k), BF16)],
        compiler_params=_params(2),
        name=name,
    )(*args)


A_HIST = 32
B_HIST = 8


def _conv_body(av_ref, ag_ref, bg_ref, cg_ref, v_ref, ctxa_ref, ctxb_ref, caw_ref, cab_ref,
               lng_ref, lnb_ref, cbw_ref, y_ref, ta_ref, tb_ref, sa_ref, sb_ref, *, tt, d_a):
    t = pl.program_id(1)

    @pl.when(t == 0)
    def _():
        sa_ref[0:A_HIST, :] = ctxa_ref[...]
        sb_ref[0:B_HIST, :] = ctxb_ref[...]

    @pl.when(t > 0)
    def _():
        sa_ref[0:A_HIST, :] = sa_ref[tt:tt + A_HIST, :]
        sb_ref[0:B_HIST, :] = sb_ref[tt:tt + B_HIST, :]

    sa_ref[A_HIST:A_HIST + tt, :] = av_ref[...] * jax.nn.sigmoid(ag_ref[...])
    sb_ref[B_HIST:B_HIST + tt, :] = cg_ref[...] * v_ref[...]

    off_a = A_HIST - (CONV_A_WIDTH - 1)
    acc = jnp.broadcast_to(cab_ref[...], (tt, d_a))
    for k in range(CONV_A_WIDTH):
        acc = acc + caw_ref[k:k + 1, :] * sa_ref[off_a + k:off_a + k + tt, :]
    mu = jnp.mean(acc, axis=-1, keepdims=True)
    xc = acc - mu
    var = jnp.mean(xc * xc, axis=-1, keepdims=True)
    ln = xc * lax.rsqrt(var + EPS) * lng_ref[...] + lnb_ref[...]
    y_ref[:, 0:d_a] = (ln * jax.nn.sigmoid(ln)).astype(y_ref.dtype)

    off_b = B_HIST - (CONV_B_WIDTH - 1)
    accb = cbw_ref[0:1, :] * sb_ref[off_b:off_b + tt, :]
    for k in range(1, CONV_B_WIDTH):
        accb = accb + cbw_ref[k:k + 1, :] * sb_ref[off_b + k:off_b + k + tt, :]
    y_ref[:, d_a:] = (bg_ref[...] * accb).astype(y_ref.dtype)

    ta_ref[...] = sa_ref[tt:tt + A_HIST, :]
    tb_ref[...] = sb_ref[tt:tt + B_HIST, :]


def _conv_mix(z, ctx_a, ctx_b, ca_w, ca_b, ln_g, ln_b, cb_w):
    nb, t, _ = z.shape
    d_a = ca_w.shape[-1]
    tt = min(t, 256)
    ctx_a = jnp.pad(ctx_a, ((0, 0), (A_HIST - ctx_a.shape[1], 0), (0, 0)))
    ctx_b = jnp.pad(ctx_b, ((0, 0), (B_HIST - ctx_b.shape[1], 0), (0, 0)))
    caw = jnp.pad(ca_w, ((0, A_HIST - CONV_A_WIDTH), (0, 0)))
    cbw = jnp.pad(cb_w, ((0, B_HIST - CONV_B_WIDTH), (0, 0)))

    def zspec(col):
        return pl.BlockSpec((None, tt, d_a), lambda b, i: (b, i, col))

    def vec(rows):
        return pl.BlockSpec((rows, d_a), lambda b, i: (0, 0))

    y, ta, tb = pl.pallas_call(
        functools.partial(_conv_body, tt=tt, d_a=d_a),
        grid=(nb, t // tt),
        in_specs=[zspec(0), zspec(1), zspec(2), zspec(3), zspec(4),
                  pl.BlockSpec((None, A_HIST, d_a), lambda b, i: (b, 0, 0)),
                  pl.BlockSpec((None, B_HIST, d_a), lambda b, i: (b, 0, 0)),
                  vec(A_HIST), vec(1), vec(1), vec(1), vec(B_HIST)],
        out_specs=[pl.BlockSpec((None, tt, 2 * d_a), lambda b, i: (b, i, 0)),
                   pl.BlockSpec((None, A_HIST, d_a), lambda b, i: (b, 0, 0)),
                   pl.BlockSpec((None, B_HIST, d_a), lambda b, i: (b, 0, 0))],
        out_shape=[jax.ShapeDtypeStruct((nb, t, 2 * d_a), BF16),
                   jax.ShapeDtypeStruct((nb, A_HIST, d_a), F32),
                   jax.ShapeDtypeStruct((nb, B_HIST, d_a), F32)],
        scratch_shapes=[pltpu.VMEM((tt + A_HIST, d_a), F32),
                        pltpu.VMEM((tt + B_HIST, d_a), F32)],
        compiler_params=_params(2),
        name="conv_mix",
    )(z, z, z, z, z, ctx_a, ctx_b, caw, ca_b.reshape(1, d_a), ln_g.reshape(1, d_a),
      ln_b.reshape(1, d_a), cbw)
    return (y, ta[:, A_HIST - (CONV_A_WIDTH - 1):], tb[:, B_HIST - (CONV_B_WIDTH - 1):])


def _qlat_body(q_ref, w_ref, o_ref, *, tm):
    for h in range(MLA_HEADS):
        a = q_ref[:, h * QK_NOPE:(h + 1) * QK_NOPE].astype(BF16)
        r = jnp.dot(a, w_ref[h].astype(BF16), preferred_element_type=F32).astype(BF16)
        for j in range(tm // CHUNK):
            o_ref[j, h] = r[j * CHUNK:(j + 1) * CHUNK]


def _mla_qlat(q, w_uk_t):
    m = q.shape[0]
    tm = min(m, 512)
    return pl.pallas_call(
        functools.partial(_qlat_body, tm=tm),
        grid=(m // tm,),
        in_specs=[pl.BlockSpec((tm, MLA_HEADS * QK_NOPE), lambda i: (i, 0)),
                  pl.BlockSpec((MLA_HEADS, QK_NOPE, KV_LORA), lambda i: (0, 0, 0))],
        out_specs=pl.BlockSpec((tm // CHUNK, MLA_HEADS, CHUNK, KV_LORA), lambda i: (i, 0, 0, 0)),
        out_shape=jax.ShapeDtypeStruct((m // CHUNK, MLA_HEADS, CHUNK, KV_LORA), BF16),
        compiler_params=_params(1),
        name="mla_qlat",
    )(q, w_uk_t)


def _mla_post_body(z_ref, qr_ref, cos_ref, sin_ref, g_ref, c32_ref, cbf_ref, kr32_ref, krbf_ref,
                   qro_ref, *, tm):
    z = z_ref[...]
    c = _rms(z[:, Q_LORA:Q_LORA + KV_LORA], g_ref[...])
    c32_ref[...] = c
    cbf_ref[...] = c.astype(BF16)

    cos = cos_ref[...]
    sin = sin_ref[...]
    kr = z[:, Q_LORA + KV_LORA:]
    half = QK_ROPE // 2
    kr_sw = jnp.concatenate([kr[:, half:], kr[:, :half]], axis=1)
    kr_rot = kr * cos[:, :QK_ROPE] + kr_sw * sin[:, :QK_ROPE]
    kr32_ref[...] = kr_rot
    krbf_ref[...] = kr_rot.astype(BF16)

    q = qr_ref[...]
    width = q.shape[1]
    lane = lax.broadcasted_iota(jnp.int32, q.shape, 1)
    first_half = (lane % QK_ROPE) < half
    q_sw = jnp.where(first_half, pltpu.roll(q, width - half, axis=1), pltpu.roll(q, half, axis=1))
    reps = width // LANES
    cos_t = jnp.concatenate([cos] * reps, axis=1)
    sin_t = jnp.concatenate([sin] * reps, axis=1)
    q_rot = q * cos_t + q_sw * sin_t
    for h in range(MLA_HEADS):
        piece = q_rot[:, h * QK_ROPE:(h + 1) * QK_ROPE].astype(BF16)
        for j in range(tm // CHUNK):
            qro_ref[j, h] = piece[j * CHUNK:(j + 1) * CHUNK]


def _mla_post(z, q, cos, sin, kv_norm):
    m = z.shape[0]
    tm = min(m, 512)
    rope_w = MLA_HEADS * QK_ROPE
    rope_block = (MLA_HEADS * QK_NOPE) // rope_w
    return pl.pallas_call(
        functools.partial(_mla_post_body, tm=tm),
        grid=(m // tm,),
        in_specs=[pl.BlockSpec((tm, z.shape[1]), lambda i: (i, 0)),
                  pl.BlockSpec((tm, rope_w), lambda i: (i, rope_block)),
                  pl.BlockSpec((tm, LANES), lambda i: (i, 0)),
                  pl.BlockSpec((tm, LANES), lambda i: (i, 0)),
                  pl.BlockSpec((1, KV_LORA), lambda i: (0, 0))],
        out_specs=[pl.BlockSpec((tm, KV_LORA), lambda i: (i, 0)),
                   pl.BlockSpec((tm, KV_LORA), lambda i: (i, 0)),
                   pl.BlockSpec((tm, QK_ROPE), lambda i: (i, 0)),
                   pl.BlockSpec((tm, QK_ROPE), lambda i: (i, 0)),
                   pl.BlockSpec((tm // CHUNK, MLA_HEADS, CHUNK, QK_ROPE), lambda i: (i, 0, 0, 0))],
        out_shape=[jax.ShapeDtypeStruct((m, KV_LORA), F32),
                   jax.ShapeDtypeStruct((m, KV_LORA), BF16),
                   jax.ShapeDtypeStruct((m, QK_ROPE), F32),
                   jax.ShapeDtypeStruct((m, QK_ROPE), BF16),
                   jax.ShapeDtypeStruct((m // CHUNK, MLA_HEADS, CHUNK, QK_ROPE), BF16)],
        compiler_params=_params(1),
        name="mla_post",
    )(z, q, cos, sin, kv_norm.reshape(1, KV_LORA))


def _rope_tables(pos):
    inv = 1.0 / (ROPE_THETA ** (jnp.arange(0, QK_ROPE, 2, dtype=F32) / QK_ROPE))
    ang = pos.astype(F32)[:, None] * inv[None, :]
    cos, sin = jnp.cos(ang), jnp.sin(ang)
    return (jnp.concatenate([cos, cos, cos, cos], axis=1),
            jnp.concatenate([-sin, sin, -sin, sin], axis=1))


MLA_KB = 512


def _mla_attn_body(ql_ref, qr_ref, c_ref, kr_ref, o_ref, s_ref, m_ref, l_ref, acc_ref):
    ci = pl.program_id(1)
    rows = MLA_HEADS * CHUNK
    ql = ql_ref[...].reshape(rows, KV_LORA)
    qr = qr_ref[...].reshape(rows, QK_ROPE)
    n_valid = (ci + 1) * CHUNK
    n_full = n_valid // MLA_KB
    tiles = MLA_KB // LANES
    m_ref[...] = jnp.full((rows, LANES), NEG_BIG, F32)

    def scores(kb, n_keep):
        k0 = pl.multiple_of(kb * MLA_KB, MLA_KB)
        s = (_dot_nt(ql, c_ref[pl.ds(k0, MLA_KB), :])
             + _dot_nt(qr, kr_ref[pl.ds(k0, MLA_KB), :])) * MLA_SCALE
        if n_keep is not None:
            col = lax.broadcasted_iota(jnp.int32, s.shape, 1)
            s = jnp.where(col < n_keep, s, NEG_BIG)
        s_ref[kb] = s
        mp = m_ref[...]
        for t in range(tiles):
            mp = jnp.maximum(mp, s[:, t * LANES:(t + 1) * LANES])
        m_ref[...] = mp

    def pass1(kb, carry):
        scores(kb, None)
        return carry

    lax.fori_loop(0, n_full, pass1, 0)
    rem = n_valid - n_full * MLA_KB

    @pl.when(rem > 0)
    def _():
        scores(n_full, rem)

    m = jnp.max(m_ref[...], axis=-1, keepdims=True)
    m_ref[...] = jnp.broadcast_to(m, (rows, LANES))
    l_ref[...] = jnp.zeros((rows, LANES), F32)
    acc_ref[...] = jnp.zeros((rows, KV_LORA), F32)

    def pass2(kb, carry):
        k0 = pl.multiple_of(kb * MLA_KB, MLA_KB)
        m_rep = m_ref[...]
        p = jnp.exp(s_ref[kb] - jnp.concatenate([m_rep] * tiles, axis=1))
        lp = l_ref[...]
        for t in range(tiles):
            lp = lp + p[:, t * LANES:(t + 1) * LANES]
        l_ref[...] = lp
        acc_ref[...] += jnp.dot(p.astype(BF16), c_ref[pl.ds(k0, MLA_KB), :],
                                preferred_element_type=F32)
        return carry

    lax.fori_loop(0, (n_valid + MLA_KB - 1) // MLA_KB, pass2, 0)
    l = jnp.sum(l_ref[...], axis=-1, keepdims=True)
    o_ref[...] = (acc_ref[...] / l).astype(o_ref.dtype).reshape(MLA_HEADS, CHUNK, KV_LORA)


def _mla_prompt_attn(qlat, qrope, cbf, krbf):
    nb, t, _ = cbf.shape
    nc = t // CHUNK
    rows = MLA_HEADS * CHUNK
    return pl.pallas_call(
        _mla_attn_body,
        grid=(nb, nc),
        in_specs=[pl.BlockSpec((None, MLA_HEADS, CHUNK, KV_LORA), lambda b, i: (b * nc + i, 0, 0, 0)),
                  pl.BlockSpec((None, MLA_HEADS, CHUNK, QK_ROPE), lambda b, i: (b * nc + i, 0, 0, 0)),
                  pl.BlockSpec((None, t, KV_LORA), lambda b, i: (b, 0, 0)),
                  pl.BlockSpec((None, t, QK_ROPE), lambda b, i: (b, 0, 0))],
        out_specs=pl.BlockSpec((None, MLA_HEADS, CHUNK, KV_LORA), lambda b, i: (b * nc + i, 0, 0, 0)),
        out_shape=jax.ShapeDtypeStruct(qlat.shape, BF16),
        scratch_shapes=[pltpu.VMEM((t // MLA_KB, rows, MLA_KB), F32),
                        pltpu.VMEM((rows, LANES), F32), pltpu.VMEM((rows, LANES), F32),
                        pltpu.VMEM((rows, KV_LORA), F32)],
        compiler_params=_params(2),
        name="mla_prompt_attn",
    )(qlat, qrope, cbf, krbf)


def _mla_dec_body(ql_ref, qr_ref, cc_ref, ckr_ref, cn_ref, krn_ref, o_ref, *, n_new):
    rows = MLA_HEADS * n_new
    ql = ql_ref[...].reshape(rows, KV_LORA)
    qr = qr_ref[...].reshape(rows, QK_ROPE)
    cc = cc_ref[...].astype(BF16)
    cn = cn_ref[...].astype(BF16)
    s1 = (_dot_nt(ql, cc) + _dot_nt(qr, ckr_ref[...].astype(BF16))) * MLA_SCALE
    s2 = (_dot_nt(ql, cn) + _dot_nt(qr, krn_ref[...].astype(BF16))) * MLA_SCALE
    col = lax.broadcasted_iota(jnp.int32, s2.shape, 1)
    s2 = jnp.where(col < n_new, s2, NEG_BIG)
    m = jnp.maximum(jnp.max(s1, axis=-1, keepdims=True), jnp.max(s2, axis=-1, keepdims=True))
    p1 = jnp.exp(s1 - m)
    p2 = jnp.exp(s2 - m)
    l = jnp.sum(p1, axis=-1, keepdims=True) + jnp.sum(p2, axis=-1, keepdims=True)
    o = (jnp.dot(p1.astype(BF16), cc, preferred_element_type=F32)
         + jnp.dot(p2.astype(BF16), cn, preferred_element_type=F32))
    o_ref[...] = (o / l).astype(o_ref.dtype).reshape(MLA_HEADS, n_new, KV_LORA)


def _mla_decode_attn(qlat, qrope, cache_c, cache_kr, c_new, kr_new):
    nb, past, _ = cache_c.shape
    n_new = c_new.shape[1]
    per_chunk = CHUNK // n_new
    c_pad = jnp.pad(c_new, ((0, 0), (0, LANES - n_new), (0, 0)))
    kr_pad = jnp.pad(kr_new, ((0, 0), (0, LANES - n_new), (0, 0)))

    def qspec(width):
        return pl.BlockSpec((None, MLA_HEADS, n_new, width),
                            lambda b: (b // per_chunk, 0, b % per_chunk, 0))

    return pl.pallas_call(
        functools.partial(_mla_dec_body, n_new=n_new),
        grid=(nb,),
        in_specs=[qspec(KV_LORA), qspec(QK_ROPE),
                  pl.BlockSpec((None, past, KV_LORA), lambda b: (b, 0, 0)),
                  pl.BlockSpec((None, past, QK_ROPE), lambda b: (b, 0, 0)),
                  pl.BlockSpec((None, LANES, KV_LORA), lambda b: (b, 0, 0)),
                  pl.BlockSpec((None, LANES, QK_ROPE), lambda b: (b, 0, 0))],
        out_specs=qspec(KV_LORA),
        out_shape=jax.ShapeDtypeStruct(qlat.shape, BF16),
        compiler_params=_params(1),
        name="mla_decode_attn",
    )(qlat, qrope, cache_c, cache_kr, c_pad, kr_pad)


def _uv_body(o_ref, w_ref, out_ref, *, tm):
    for h in range(MLA_HEADS):
        a = jnp.concatenate([o_ref[j, h] for j in range(tm // CHUNK)], axis=0)
        out_ref[:, h * V_DIM:(h + 1) * V_DIM] = jnp.dot(
            a, w_ref[h].astype(BF16), preferred_element_type=F32).astype(out_ref.dtype)


def _mla_uv(o_lat, w_uv_t):
    m = o_lat.shape[0] * CHUNK
    tm = min(m, 512)
    return pl.pallas_call(
        functools.partial(_uv_body, tm=tm),
        grid=(m // tm,),
        in_specs=[pl.BlockSpec((tm // CHUNK, MLA_HEADS, CHUNK, KV_LORA), lambda i: (i, 0, 0, 0)),
                  pl.BlockSpec((MLA_HEADS, KV_LORA, V_DIM), lambda i: (0, 0, 0))],
        out_specs=pl.BlockSpec((tm, MLA_HEADS * V_DIM), lambda i: (i, 0)),
        out_shape=jax.ShapeDtypeStruct((m, MLA_HEADS * V_DIM), BF16),
        compiler_params=_params(1),
        name="mla_uv",
    )(o_lat, w_uv_t)


def _xattn_body(q_ref, k_ref, v_ref, o_ref, *, heads, hd):
    scale = hd ** -0.5
    for h in range(heads):
        sl = slice(h * hd, (h + 1) * hd)
        q = q_ref[:, sl].astype(BF16)
        k = k_ref[:, sl].astype(BF16)
        v = v_ref[:, sl].astype(BF16)
        s = _dot_nt(q, k) * scale
        e = jnp.exp(s - jnp.max(s, axis=-1, keepdims=True))
        p = e / jnp.sum(e, axis=-1, keepdims=True)
        o_ref[:, sl] = jnp.dot(p.astype(BF16), v, preferred_element_type=F32).astype(o_ref.dtype)


def _xattn(q, k, v):
    nb, t, d = q.shape
    n_mem = k.shape[1]
    tt = min(t, 512)
    return pl.pallas_call(
        functools.partial(_xattn_body, heads=XA_HEADS, hd=d // XA_HEADS),
        grid=(nb, t // tt),
        in_specs=[pl.BlockSpec((None, tt, d), lambda b, i: (b, i, 0)),
                  pl.BlockSpec((None, n_mem, d), lambda b, i: (b, 0, 0)),
                  pl.BlockSpec((None, n_mem, d), lambda b, i: (b, 0, 0))],
        out_specs=pl.BlockSpec((None, tt, d), lambda b, i: (b, i, 0)),
        out_shape=jax.ShapeDtypeStruct((nb, t, d), BF16),
        compiler_params=_params(2),
        name="xattn",
    )(q, k, v)


ROUTE_TN = LANES
STAT_ROWS = 8


def _sort_network(n):
    pairs = []
    p = 1
    while p < n:
        k = p
        while k >= 1:
            for j in range(k % p, n - k, 2 * k):
                for i in range(min(k, n - j - k)):
                    if (i + j) // (2 * p) == (i + j + k) // (2 * p):
                        pairs.append((i + j, i + j + k))
            k //= 2
        p *= 2
    return pairs


def _sorted_top(problems, n, rows_out):
    size = 1
    while size < max(len(p) for p in problems):
        size *= 2
    net = _sort_network(size)
    problems = [list(p) for p in problems]
    for i, j in net:
        for p in problems:
            if j < len(p):
                hi = jnp.maximum(p[i], p[j])
                p[j] = jnp.minimum(p[i], p[j])
                p[i] = hi
    outs = [[] for _ in problems]
    for r in range(n):
        for idx, p in enumerate(problems):
            p = p[:n - r]
            m = jnp.max(p[0], axis=0, keepdims=True)
            outs[idx].append(m)
            if r + 1 < n:
                hit = p[0] == m
                p = ([jnp.where(hit, p[k + 1], p[k]) for k in range(len(p) - 1)]
                     + [jnp.where(hit, -jnp.inf, p[-1])])
            problems[idx] = p
    lanes = problems[0][0].shape[1]
    pad = [jnp.full((rows_out - n, lanes), -jnp.inf, F32)]
    return [jnp.concatenate(o + pad, axis=0) for o in outs]


def _pieces(x):
    return [x[k:k + SUBLANES] for k in range(0, x.shape[0], SUBLANES)]


ROUTE_N = PEER_TOPK + 1
ROUTE_ROWS = 24


def _pair_sum_candidates(a, b):
    g = SUBLANES
    pieces = _pieces(a[0:1] + b)
    pieces += [a[r:r + 1] + b[0:g] for r in range(1, g)]
    pieces += _pieces(a[g:ROUTE_ROWS] + b[0:1])
    return pieces


def _route_body(q_ref, sk_ref, s0_ref, s1_ref, st_ref):
    sk0 = sk_ref[0].astype(BF16)
    sk1 = sk_ref[1].astype(BF16)
    half = sk_ref.shape[2]

    def head_scores(h):
        c0 = pl.multiple_of(h * 2 * half, LANES)
        c1 = pl.multiple_of(h * 2 * half + half, LANES)
        s0 = _dot_nt(sk0, q_ref[:, pl.ds(c0, half)].astype(BF16))
        s1 = _dot_nt(sk1, q_ref[:, pl.ds(c1, half)].astype(BF16))
        s0_ref[h] = s0
        s1_ref[h] = s1
        return _pieces(s0), _pieces(s1)

    def write_stats(h, a, b, cand, top):
        tau = 0.5 * (top[PEER_TOPK - 1:PEER_TOPK] + top[PEER_TOPK:PEER_TOPK + 1])
        zp = None
        for c in cand:
            t = jnp.where(c >= tau, jnp.exp(c - top[0:1]), 0.0)
            zp = t if zp is None else zp + t
        z = jnp.sum(zp, axis=0, keepdims=True)
        st_ref[h] = jnp.concatenate(
            [tau, a[0:1], b[0:1], 1.0 / z, jnp.zeros((STAT_ROWS - 4, ROUTE_TN), F32)], axis=0)

    def head_pair(hp, carry):
        h0 = hp * 2
        p00, p01 = head_scores(h0)
        p10, p11 = head_scores(h0 + 1)
        a0, b0, a1, b1 = _sorted_top([p00, p01, p10, p11], ROUTE_N, ROUTE_ROWS)
        cand0 = _pair_sum_candidates(a0, b0)
        cand1 = _pair_sum_candidates(a1, b1)
        top0, top1 = _sorted_top([cand0, cand1], ROUTE_N, ROUTE_ROWS)
        write_stats(h0, a0, b0, cand0, top0)
        write_stats(h0 + 1, a1, b1, cand1, top1)
        return carry

    lax.fori_loop(0, PEER_HEADS // 2, head_pair, 0)


def _peer_route(q, subkeys):
    n = q.shape[0]
    half = subkeys.shape[2]
    spec_s = pl.BlockSpec((PEER_HEADS, N_KEYS, ROUTE_TN), lambda i: (0, 0, i))
    return pl.pallas_call(
        _route_body,
        grid=(n // ROUTE_TN,),
        in_specs=[pl.BlockSpec((ROUTE_TN, q.shape[1]), lambda i: (i, 0)),
                  pl.BlockSpec((2, N_KEYS, half), lambda i: (0, 0, 0))],
        out_specs=[spec_s, spec_s,
                   pl.BlockSpec((PEER_HEADS, STAT_ROWS, ROUTE_TN), lambda i: (0, 0, i))],
        out_shape=[jax.ShapeDtypeStruct((PEER_HEADS, N_KEYS, n), F32),
                   jax.ShapeDtypeStruct((PEER_HEADS, N_KEYS, n), F32),
                   jax.ShapeDtypeStruct((PEER_HEADS, STAT_ROWS, n), F32)],
        compiler_params=_params(1),
        name="peer_route",
    )(q, subkeys)


PEER_EB = 1024


def _gelu_tanh(x):
    c = 0.7978845608028654
    t = jnp.tanh(x * (c + (c * 0.044715) * (x * x)))
    hx = 0.5 * x
    return hx + hx * t


def _peer_body(x_ref, g_ref, u_ref, v_ref, s0_ref, s1_ref, st_ref, *rest, tn, final_norm):
    fg_ref = rest[0] if final_norm else None
    o_ref, xn_ref, e1_ref, w_ref = rest[-4:]
    @pl.when(pl.program_id(1) == 0)
    def _():
        xn_ref[...] = _rms(x_ref[...], g_ref[...]).astype(BF16)
        for h in range(PEER_HEADS):
            e1_ref[h] = jnp.exp(s1_ref[h] - st_ref[2, h:h + 1, :])
        o_ref[...] = x_ref[...]

    act = _dot_nt(u_ref[...], xn_ref[...])
    for c in range(tn // LANES):
        lanes = slice(c * LANES, (c + 1) * LANES)
        for ii in range(PEER_EB // N_KEYS):
            s0rows = s0_ref[ii, :, lanes]
            thr_all = st_ref[0, :, lanes] - s0rows
            coef_all = jnp.exp(s0rows - st_ref[1, :, lanes]) * st_ref[3, :, lanes]
            g = None
            for h in range(PEER_HEADS):
                gh = jnp.where(s1_ref[h, :, lanes] >= thr_all[h:h + 1],
                               e1_ref[h, :, lanes] * coef_all[h:h + 1], 0.0)
                g = gh if g is None else g + gh
            a = act[ii * N_KEYS:(ii + 1) * N_KEYS, lanes]
            w_ref[ii * N_KEYS:(ii + 1) * N_KEYS, lanes] = (_gelu_tanh(a) * g).astype(BF16)
    o_ref[...] += lax.dot_general(w_ref[...], v_ref[...], (((0,), (0,)), ((), ())),
                                  preferred_element_type=F32)

    if fg_ref is not None:
        @pl.when(pl.program_id(1) == pl.num_programs(1) - 1)
        def _():
            o_ref[...] = _rms(o_ref[...], fg_ref[...])


def _peer_dense(x, g, u_tab, v_tab, s0t, s1t, stats, final_g=None):
    n, d = x.shape
    n_exp = u_tab.shape[0]
    tn = min(n, 512)
    groups = PEER_EB // N_KEYS
    s0_by_key = jnp.transpose(s0t, (1, 0, 2))
    stats_by_kind = jnp.transpose(stats, (1, 0, 2))
    extra_specs = [] if final_g is None else [pl.BlockSpec((1, d), lambda i, e: (0, 0))]
    extra_args = [] if final_g is None else [final_g.reshape(1, d)]
    return pl.pallas_call(
        functools.partial(_peer_body, tn=tn, final_norm=final_g is not None),
        grid=(n // tn, n_exp // PEER_EB),
        in_specs=[pl.BlockSpec((tn, d), lambda i, e: (i, 0)),
                  pl.BlockSpec((1, d), lambda i, e: (0, 0)),
                  pl.BlockSpec((PEER_EB, d), lambda i, e: (e, 0)),
                  pl.BlockSpec((PEER_EB, d), lambda i, e: (e, 0)),
                  pl.BlockSpec((groups, PEER_HEADS, tn), lambda i, e: (e, 0, i)),
                  pl.BlockSpec((PEER_HEADS, N_KEYS, tn), lambda i, e: (0, 0, i)),
                  pl.BlockSpec((STAT_ROWS, PEER_HEADS, tn), lambda i, e: (0, 0, i))] + extra_specs,
        out_specs=pl.BlockSpec((tn, d), lambda i, e: (i, 0)),
        out_shape=jax.ShapeDtypeStruct((n, d), F32),
        scratch_shapes=[pltpu.VMEM((tn, d), BF16),
                        pltpu.VMEM((PEER_HEADS, N_KEYS, tn), F32),
                        pltpu.VMEM((PEER_EB, tn), BF16)],
        compiler_params=_params(2),
        name="peer_dense",
    )(x, g.reshape(1, d), u_tab, v_tab, s0_by_key, s1t, stats_by_kind, *extra_args)


def _conv_layer(x, nb, t, ctx_a, ctx_b, g, w_in, ca_w, ca_b, ln_g, ln_b, cb_w, w_out):
    z = _norm_matmul(x, w_in, g, name="conv_in")
    y, new_a, new_b = _conv_mix(z.reshape(nb, t, -1), ctx_a, ctx_b, ca_w, ca_b, ln_g, ln_b, cb_w)
    x = _norm_matmul(y.reshape(nb * t, -1), w_out, res=x, name="conv_out")
    return x, new_a, new_b


def _mla_project(x, pos, g, w_down, q_norm, kv_norm, w_uq_perm, w_uk_t):
    z = _norm_matmul(x, w_down, g, name="mla_down")
    q = _norm_matmul(z, w_uq_perm, q_norm, name="mla_uq")
    qlat = _mla_qlat(q, w_uk_t)
    cos, sin = _rope_tables(pos)
    c32, cbf, kr32, krbf, qrope = _mla_post(z, q, cos, sin, kv_norm)
    return qlat, qrope, c32, cbf, kr32, krbf


def _mla_finish(x, o_lat, w_uv_t, w_o):
    return _norm_matmul(_mla_uv(o_lat, w_uv_t), w_o, res=x, name="mla_o")


def _xattn_layer(x, nb, t, k, v, g, wq, wo, layer):
    d = x.shape[1]
    q = _norm_matmul(x, wq, g, layer=layer, name="xa_q")
    o = _xattn(q.reshape(nb, t, d), k, v)
    return _norm_matmul(o.reshape(nb * t, d), wo, res=x, layer=layer, name="xa_o")


def _peer_layer(x, g, w_q, layer, subkeys, u_tab, v_tab, final_g=None):
    q = _norm_matmul(x, w_q, g, layer=layer, name="peer_q")
    s0t, s1t, stats = _peer_route(q, subkeys)
    return _peer_dense(x, g, u_tab, v_tab, s0t, s1t, stats, final_g)


def kernel(x_prompt, x_sample, cache_conv_a, cache_conv_b, cache_mla_latent, cache_mla_krope, cache_mem_k, cache_mem_v, mem_prompt, norm_mix, w_conv_in, conv_a_w, conv_a_b, ln_a_g, ln_a_b, conv_b_w, w_conv_out, w_mla_down, mla_q_norm, mla_kv_norm, w_mla_uq, w_mla_uk, w_mla_uv, w_mla_o, norm_xa, norm_mem, w_xa_q, w_xa_k, w_xa_v, w_xa_o, norm_ffn, peer_w_q, peer_subkeys, peer_u, peer_v, norm_final):
    bp, tp, d = x_prompt.shape
    bs, ts, _ = x_sample.shape
    depth = norm_mix.shape[0]
    past = cache_mla_latent.shape[2]
    n_mem = mem_prompt.shape[1]
    xp = x_prompt.reshape(bp * tp, d)
    xs = x_sample.reshape(bs * ts, d)
    pos_p = jnp.tile(jnp.arange(tp, dtype=jnp.int32), bp)
    pos_s = jnp.tile(past + jnp.arange(ts, dtype=jnp.int32), bs)
    mem2d = mem_prompt.reshape(bp * n_mem, d)

    conv_a_p, conv_b_p, lat_p, kr_p, mem_k_p, mem_v_p = [], [], [], [], [], []
    conv_a_s, conv_b_s, lat_s, kr_s = [], [], [], []
    for l in range(depth):
        j = l // 2
        if l % 2 == 0:
            d_a = conv_a_w.shape[-1]
            zero_a = jnp.zeros((bp, CONV_A_WIDTH - 1, d_a), F32)
            zero_b = jnp.zeros((bp, CONV_B_WIDTH - 1, conv_b_w.shape[-1]), F32)
            wts = (norm_mix[l], w_conv_in[j], conv_a_w[j], conv_a_b[j], ln_a_g[j], ln_a_b[j],
                   conv_b_w[j], w_conv_out[j])
            xp, sa, sb = _conv_layer(xp, bp, tp, zero_a, zero_b, *wts)
            conv_a_p.append(sa)
            conv_b_p.append(sb)
            xs, sa, sb = _conv_layer(xs, bs, ts, cache_conv_a[j], cache_conv_b[j], *wts)
            conv_a_s.append(sa)
            conv_b_s.append(sb)
        else:
            uq = w_mla_uq[j].reshape(Q_LORA, MLA_HEADS, QK_NOPE + QK_ROPE)
            w_uq_perm = jnp.concatenate(
                [uq[:, :, :QK_NOPE].reshape(Q_LORA, MLA_HEADS * QK_NOPE),
                 uq[:, :, QK_NOPE:].reshape(Q_LORA, MLA_HEADS * QK_ROPE)], axis=1)
            w_uk_t = jnp.transpose(w_mla_uk[j], (1, 2, 0))
            w_uv_t = jnp.transpose(w_mla_uv[j], (1, 0, 2))
            proj = (norm_mix[l], w_mla_down[j], mla_q_norm[j], mla_kv_norm[j], w_uq_perm, w_uk_t)

            ql, qr, c32, cbf, kr32, krbf = _mla_project(xp, pos_p, *proj)
            o_lat = _mla_prompt_attn(ql, qr, cbf.reshape(bp, tp, KV_LORA),
                                     krbf.reshape(bp, tp, QK_ROPE))
            xp = _mla_finish(xp, o_lat, w_uv_t, w_mla_o[j])
            lat_p.append(c32.reshape(bp, tp, KV_LORA))
            kr_p.append(kr32.reshape(bp, tp, QK_ROPE))

            ql, qr, c32, cbf, kr32, krbf = _mla_project(xs, pos_s, *proj)
            c_new = c32.reshape(bs, ts, KV_LORA)
            kr_new = kr32.reshape(bs, ts, QK_ROPE)
            o_lat = _mla_decode_attn(ql, qr, cache_mla_latent[j], cache_mla_krope[j], c_new, kr_new)
            xs = _mla_finish(xs, o_lat, w_uv_t, w_mla_o[j])
            lat_s.append(c_new)
            kr_s.append(kr_new)

        mk = _norm_matmul(mem2d, w_xa_k, norm_mem[l], layer=l, name="mem_k")
        mv = _norm_matmul(mem2d, w_xa_v, norm_mem[l], layer=l, name="mem_v")
        hd = d // XA_HEADS
        mem_k_p.append(mk.reshape(bp, n_mem, XA_HEADS, hd))
        mem_v_p.append(mv.reshape(bp, n_mem, XA_HEADS, hd))
        xp = _xattn_layer(xp, bp, tp, mk.reshape(bp, n_mem, d), mv.reshape(bp, n_mem, d),
                          norm_xa[l], w_xa_q, w_xa_o, l)
        xs = _xattn_layer(xs, bs, ts, cache_mem_k[l].reshape(bs, n_mem, d),
                          cache_mem_v[l].reshape(bs, n_mem, d), norm_xa[l], w_xa_q, w_xa_o, l)

        u_tab = peer_u[l].astype(BF16)
        v_tab = peer_v[l].astype(BF16)
        final_g = norm_final if l == depth - 1 else None
        xp = _peer_layer(xp, norm_ffn[l], peer_w_q, l, peer_subkeys[l], u_tab, v_tab, final_g)
        xs = _peer_layer(xs, norm_ffn[l], peer_w_q, l, peer_subkeys[l], u_tab, v_tab, final_g)

    y_prompt = xp.reshape(bp, tp, d)
    y_sample = xs.reshape(bs, ts, d)
    return (y_prompt, y_sample,
            jnp.stack(conv_a_p), jnp.stack(conv_b_p), jnp.stack(lat_p), jnp.stack(kr_p),
            jnp.stack(mem_k_p), jnp.stack(mem_v_p),
            jnp.stack(conv_a_s), jnp.stack(conv_b_s), jnp.stack(lat_s), jnp.stack(kr_s))
```

```python
import functools

import jax
import jax.numpy as jnp
from jax import lax
from jax.experimental import pallas as pl
from jax.experimental.pallas import tpu as pltpu

F32 = jnp.float32
BF16 = jnp.bfloat16

EPS = 1e-6
CHUNK = 64
CONV_A_WIDTH = 31
CONV_B_WIDTH = 3
MLA_HEADS = 16
Q_LORA = 512
KV_LORA = 512
QK_NOPE = 128
QK_ROPE = 64
V_DIM = 128
ROPE_THETA = 10000.0
MLA_SCALE = (QK_NOPE + QK_ROPE) ** -0.5
XA_HEADS = 4
PEER_HEADS = 8
N_KEYS = 128
PEER_TOPK = 16

LANES = 128
SUBLANES = 8
NEG_BIG = -1e30
MIB = 1024 * 1024
VMEM_LIMIT = 56 * MIB
RESIDENT_W_BYTES = 8 * MIB


def _params(n_axes, vmem=VMEM_LIMIT):
    return pltpu.CompilerParams(dimension_semantics=("arbitrary",) * n_axes,
                                vmem_limit_bytes=vmem)


def _dot_nt(a, b):
    return lax.dot_general(a, b, (((1,), (1,)), ((), ())), preferred_element_type=F32)


def _rms(x, g):
    return x * lax.rsqrt(jnp.mean(x * x, axis=-1, keepdims=True) + EPS) * g


def _nm_body(*refs, has_norm, has_res):
    it = iter(refs)
    x_ref = next(it)
    g_ref = next(it) if has_norm else None
    w_ref = next(it)
    r_ref = next(it) if has_res else None
    o_ref = next(it)
    xn_ref = next(it)

    @pl.when(pl.program_id(1) == 0)
    def _():
        x = x_ref[...].astype(F32)
        if has_norm:
            x = _rms(x, g_ref[...])
        xn_ref[...] = x.astype(BF16)

    acc = jnp.dot(xn_ref[...], w_ref[...].astype(BF16), preferred_element_type=F32)
    if has_res:
        acc = acc + r_ref[...]
    o_ref[...] = acc.astype(o_ref.dtype)


def _norm_matmul(x, w, g=None, res=None, *, layer=None, x_col_block=0, out_dtype=F32, name):
    m = x.shape[0]
    k, n = w.shape[-2:]
    if k * n * w.dtype.itemsize <= RESIDENT_W_BYTES:
        tm, tn = min(m, 512), n
    else:
        tm, tn = min(m, 1024), (512 if n % 512 == 0 else n)
    has_norm = g is not None
    has_res = res is not None
    in_specs = [pl.BlockSpec((tm, k), lambda i, j: (i, x_col_block))]
    args = [x]
    if has_norm:
        in_specs.append(pl.BlockSpec((1, k), lambda i, j: (0, 0)))
        args.append(g.reshape(1, k))
    if layer is None:
        in_specs.append(pl.BlockSpec((k, tn), lambda i, j: (0, j)))
    else:
        in_specs.append(pl.BlockSpec((None, k, tn), lambda i, j: (layer, 0, j)))
    args.append(w)
    if has_res:
        in_specs.append(pl.BlockSpec((tm, tn), lambda i, j: (i, j)))
        args.append(res)
    return pl.pallas_call(
        functools.partial(_nm_body, has_norm=has_norm, has_res=has_res),
        grid=(m // tm, n // tn),
        in_specs=in_specs,
        out_specs=pl.BlockSpec((tm, tn), lambda i, j: (i, j)),
        out_shape=jax.ShapeDtypeStruct((m, n), out_dtype),
        scratch_shapes=[pltpu.VMEM((tm, k), BF16)],
        compiler_params=_params(2),
        name=name,
    )(*args)


A_HIST = 32
B_HIST = 8


def _conv_body(av_ref, ag_ref, bg_ref, cg_ref, v_ref, ctxa_ref, ctxb_ref, caw_ref, cab_ref,
               lng_ref, lnb_ref, cbw_ref, y_ref, ta_ref, tb_ref, sa_ref, sb_ref, *, tt, d_a):
    t = pl.program_id(1)

    @pl.when(t == 0)
    def _():
        sa_ref[0:A_HIST, :] = ctxa_ref[...]
        sb_ref[0:B_HIST, :] = ctxb_ref[...]

    @pl.when(t > 0)
    def _():
        sa_ref[0:A_HIST, :] = sa_ref[tt:tt + A_HIST, :]
        sb_ref[0:B_HIST, :] = sb_ref[tt:tt + B_HIST, :]

    sa_ref[A_HIST:A_HIST + tt, :] = av_ref[...] * jax.nn.sigmoid(ag_ref[...])
    sb_ref[B_HIST:B_HIST + tt, :] = cg_ref[...] * v_ref[...]

    off_a = A_HIST - (CONV_A_WIDTH - 1)
    acc = jnp.broadcast_to(cab_ref[...], (tt, d_a))
    for k in range(CONV_A_WIDTH):
        acc = acc + caw_ref[k:k + 1, :] * sa_ref[off_a + k:off_a + k + tt, :]
    mu = jnp.mean(acc, axis=-1, keepdims=True)
    xc = acc - mu
    var = jnp.mean(xc * xc, axis=-1, keepdims=True)
    ln = xc * lax.rsqrt(var + EPS) * lng_ref[...] + lnb_ref[...]
    y_ref[:, 0:d_a] = (ln * jax.nn.sigmoid(ln)).astype(y_ref.dtype)

    off_b = B_HIST - (CONV_B_WIDTH - 1)
    accb = cbw_ref[0:1, :] * sb_ref[off_b:off_b + tt, :]
    for k in range(1, CONV_B_WIDTH):
        accb = accb + cbw_ref[k:k + 1, :] * sb_ref[off_b + k:off_b + k + tt, :]
    y_ref[:, d_a:] = (bg_ref[...] * accb).astype(y_ref.dtype)

    ta_ref[...] = sa_ref[tt:tt + A_HIST, :]
    tb_ref[...] = sb_ref[tt:tt + B_HIST, :]


def _conv_mix(z, ctx_a, ctx_b, ca_w, ca_b, ln_g, ln_b, cb_w):
    nb, t, _ = z.shape
    d_a = ca_w.shape[-1]
    tt = min(t, 256)
    ctx_a = jnp.pad(ctx_a, ((0, 0), (A_HIST - ctx_a.shape[1], 0), (0, 0)))
    ctx_b = jnp.pad(ctx_b, ((0, 0), (B_HIST - ctx_b.shape[1], 0), (0, 0)))
    caw = jnp.pad(ca_w, ((0, A_HIST - CONV_A_WIDTH), (0, 0)))
    cbw = jnp.pad(cb_w, ((0, B_HIST - CONV_B_WIDTH), (0, 0)))

    def zspec(col):
        return pl.BlockSpec((None, tt, d_a), lambda b, i: (b, i, col))

    def vec(rows):
        return pl.BlockSpec((rows, d_a), lambda b, i: (0, 0))

    y, ta, tb = pl.pallas_call(
        functools.partial(_conv_body, tt=tt, d_a=d_a),
        grid=(nb, t // tt),
        in_specs=[zspec(0), zspec(1), zspec(2), zspec(3), zspec(4),
                  pl.BlockSpec((None, A_HIST, d_a), lambda b, i: (b, 0, 0)),
                  pl.BlockSpec((None, B_HIST, d_a), lambda b, i: (b, 0, 0)),
                  vec(A_HIST), vec(1), vec(1), vec(1), vec(B_HIST)],
        out_specs=[pl.BlockSpec((None, tt, 2 * d_a), lambda b, i: (b, i, 0)),
                   pl.BlockSpec((None, A_HIST, d_a), lambda b, i: (b, 0, 0)),
                   pl.BlockSpec((None, B_HIST, d_a), lambda b, i: (b, 0, 0))],
        out_shape=[jax.ShapeDtypeStruct((nb, t, 2 * d_a), BF16),
                   jax.ShapeDtypeStruct((nb, A_HIST, d_a), F32),
                   jax.ShapeDtypeStruct((nb, B_HIST, d_a), F32)],
        scratch_shapes=[pltpu.VMEM((tt + A_HIST, d_a), F32),
                        pltpu.VMEM((tt + B_HIST, d_a), F32)],
        compiler_params=_params(2),
        name="conv_mix",
    )(z, z, z, z, z, ctx_a, ctx_b, caw, ca_b.reshape(1, d_a), ln_g.reshape(1, d_a),
      ln_b.reshape(1, d_a), cbw)
    return (y, ta[:, A_HIST - (CONV_A_WIDTH - 1):], tb[:, B_HIST - (CONV_B_WIDTH - 1):])


def _qlat_body(q_ref, w_ref, o_ref, *, tm):
    for h in range(MLA_HEADS):
        a = q_ref[:, h * QK_NOPE:(h + 1) * QK_NOPE].astype(BF16)
        r = jnp.dot(a, w_ref[h].astype(BF16), preferred_element_type=F32).astype(BF16)
        for j in range(tm // CHUNK):
            o_ref[j, h] = r[j * CHUNK:(j + 1) * CHUNK]


def _mla_qlat(q, w_uk_t):
    m = q.shape[0]
    tm = min(m, 512)
    return pl.pallas_call(
        functools.partial(_qlat_body, tm=tm),
        grid=(m // tm,),
        in_specs=[pl.BlockSpec((tm, MLA_HEADS * QK_NOPE), lambda i: (i, 0)),
                  pl.BlockSpec((MLA_HEADS, QK_NOPE, KV_LORA), lambda i: (0, 0, 0))],
        out_specs=pl.BlockSpec((tm // CHUNK, MLA_HEADS, CHUNK, KV_LORA), lambda i: (i, 0, 0, 0)),
        out_shape=jax.ShapeDtypeStruct((m // CHUNK, MLA_HEADS, CHUNK, KV_LORA), BF16),
        compiler_params=_params(1),
        name="mla_qlat",
    )(q, w_uk_t)


def _mla_post_body(z_ref, qr_ref, cos_ref, sin_ref, g_ref, c32_ref, cbf_ref, kr32_ref, krbf_ref,
                   qro_ref, *, tm):
    z = z_ref[...]
    c = _rms(z[:, Q_LORA:Q_LORA + KV_LORA], g_ref[...])
    c32_ref[...] = c
    cbf_ref[...] = c.astype(BF16)

    cos = cos_ref[...]
    sin = sin_ref[...]
    kr = z[:, Q_LORA + KV_LORA:]
    half = QK_ROPE // 2
    kr_sw = jnp.concatenate([kr[:, half:], kr[:, :half]], axis=1)
    kr_rot = kr * cos[:, :QK_ROPE] + kr_sw * sin[:, :QK_ROPE]
    kr32_ref[...] = kr_rot
    krbf_ref[...] = kr_rot.astype(BF16)

    q = qr_ref[...]
    width = q.shape[1]
    lane = lax.broadcasted_iota(jnp.int32, q.shape, 1)
    first_half = (lane % QK_ROPE) < half
    q_sw = jnp.where(first_half, pltpu.roll(q, width - half, axis=1), pltpu.roll(q, half, axis=1))
    reps = width // LANES
    cos_t = jnp.concatenate([cos] * reps, axis=1)
    sin_t = jnp.concatenate([sin] * reps, axis=1)
    q_rot = q * cos_t + q_sw * sin_t
    for h in range(MLA_HEADS):
        piece = q_rot[:, h * QK_ROPE:(h + 1) * QK_ROPE].astype(BF16)
        for j in range(tm // CHUNK):
            qro_ref[j, h] = piece[j * CHUNK:(j + 1) * CHUNK]


def _mla_post(z, q, cos, sin, kv_norm):
    m = z.shape[0]
    tm = min(m, 512)
    rope_w = MLA_HEADS * QK_ROPE
    rope_block = (MLA_HEADS * QK_NOPE) // rope_w
    return pl.pallas_call(
        functools.partial(_mla_post_body, tm=tm),
        grid=(m // tm,),
        in_specs=[pl.BlockSpec((tm, z.shape[1]), lambda i: (i, 0)),
                  pl.BlockSpec((tm, rope_w), lambda i: (i, rope_block)),
                  pl.BlockSpec((tm, LANES), lambda i: (i, 0)),
                  pl.BlockSpec((tm, LANES), lambda i: (i, 0)),
                  pl.BlockSpec((1, KV_LORA), lambda i: (0, 0))],
        out_specs=[pl.BlockSpec((tm, KV_LORA), lambda i: (i, 0)),
                   pl.BlockSpec((tm, KV_LORA), lambda i: (i, 0)),
                   pl.BlockSpec((tm, QK_ROPE), lambda i: (i, 0)),
                   pl.BlockSpec((tm, QK_ROPE), lambda i: (i, 0)),
                   pl.BlockSpec((tm // CHUNK, MLA_HEADS, CHUNK, QK_ROPE), lambda i: (i, 0, 0, 0))],
        out_shape=[jax.ShapeDtypeStruct((m, KV_LORA), F32),
                   jax.ShapeDtypeStruct((m, KV_LORA), BF16),
                   jax.ShapeDtypeStruct((m, QK_ROPE), F32),
                   jax.ShapeDtypeStruct((m, QK_ROPE), BF16),
                   jax.ShapeDtypeStruct((m // CHUNK, MLA_HEADS, CHUNK, QK_ROPE), BF16)],
        compiler_params=_params(1),
        name="mla_post",
    )(z, q, cos, sin, kv_norm.reshape(1, KV_LORA))


def _rope_tables(pos):
    inv = 1.0 / (ROPE_THETA ** (jnp.arange(0, QK_ROPE, 2, dtype=F32) / QK_ROPE))
    ang = pos.astype(F32)[:, None] * inv[None, :]
    cos, sin = jnp.cos(ang), jnp.sin(ang)
    return (jnp.concatenate([cos, cos, cos, cos], axis=1),
            jnp.concatenate([-sin, sin, -sin, sin], axis=1))


MLA_KB = 512


def _mla_attn_body(ql_ref, qr_ref, c_ref, kr_ref, o_ref, s_ref, m_ref, l_ref, acc_ref):
    ci = pl.program_id(1)
    rows = MLA_HEADS * CHUNK
    ql = ql_ref[...].reshape(rows, KV_LORA)
    qr = qr_ref[...].reshape(rows, QK_ROPE)
    n_valid = (ci + 1) * CHUNK
    n_full = n_valid // MLA_KB
    tiles = MLA_KB // LANES
    m_ref[...] = jnp.full((rows, LANES), NEG_BIG, F32)

    def scores(kb, n_keep):
        k0 = pl.multiple_of(kb * MLA_KB, MLA_KB)
        s = (_dot_nt(ql, c_ref[pl.ds(k0, MLA_KB), :])
             + _dot_nt(qr, kr_ref[pl.ds(k0, MLA_KB), :])) * MLA_SCALE
        if n_keep is not None:
            col = lax.broadcasted_iota(jnp.int32, s.shape, 1)
            s = jnp.where(col < n_keep, s, NEG_BIG)
        s_ref[kb] = s
        mp = m_ref[...]
        for t in range(tiles):
            mp = jnp.maximum(mp, s[:, t * LANES:(t + 1) * LANES])
        m_ref[...] = mp

    def pass1(kb, carry):
        scores(kb, None)
        return carry

    lax.fori_loop(0, n_full, pass1, 0)
    rem = n_valid - n_full * MLA_KB

    @pl.when(rem > 0)
    def _():
        scores(n_full, rem)

    m = jnp.max(m_ref[...], axis=-1, keepdims=True)
    m_ref[...] = jnp.broadcast_to(m, (rows, LANES))
    l_ref[...] = jnp.zeros((rows, LANES), F32)
    acc_ref[...] = jnp.zeros((rows, KV_LORA), F32)

    def pass2(kb, carry):
        k0 = pl.multiple_of(kb * MLA_KB, MLA_KB)
        m_rep = m_ref[...]
        p = jnp.exp(s_ref[kb] - jnp.concatenate([m_rep] * tiles, axis=1))
        lp = l_ref[...]
        for t in range(tiles):
            lp = lp + p[:, t * LANES:(t + 1) * LANES]
        l_ref[...] = lp
        acc_ref[...] += jnp.dot(p.astype(BF16), c_ref[pl.ds(k0, MLA_KB), :],
                                preferred_element_type=F32)
        return carry

    lax.fori_loop(0, (n_valid + MLA_KB - 1) // MLA_KB, pass2, 0)
    l = jnp.sum(l_ref[...], axis=-1, keepdims=True)
    o_ref[...] = (acc_ref[...] / l).astype(o_ref.dtype).reshape(MLA_HEADS, CHUNK, KV_LORA)


def _mla_prompt_attn(qlat, qrope, cbf, krbf):
    nb, t, _ = cbf.shape
    nc = t // CHUNK
    rows = MLA_HEADS * CHUNK
    return pl.pallas_call(
        _mla_attn_body,
        grid=(nb, nc),
        in_specs=[pl.BlockSpec((None, MLA_HEADS, CHUNK, KV_LORA), lambda b, i: (b * nc + i, 0, 0, 0)),
                  pl.BlockSpec((None, MLA_HEADS, CHUNK, QK_ROPE), lambda b, i: (b * nc + i, 0, 0, 0)),
                  pl.BlockSpec((None, t, KV_LORA), lambda b, i: (b, 0, 0)),
                  pl.BlockSpec((None, t, QK_ROPE), lambda b, i: (b, 0, 0))],
        out_specs=pl.BlockSpec((None, MLA_HEADS, CHUNK, KV_LORA), lambda b, i: (b * nc + i, 0, 0, 0)),
        out_shape=jax.ShapeDtypeStruct(qlat.shape, BF16),
        scratch_shapes=[pltpu.VMEM((t // MLA_KB, rows, MLA_KB), F32),
                        pltpu.VMEM((rows, LANES), F32), pltpu.VMEM((rows, LANES), F32),
                        pltpu.VMEM((rows, KV_LORA), F32)],
        compiler_params=_params(2),
        name="mla_prompt_attn",
    )(qlat, qrope, cbf, krbf)


def _mla_dec_body(ql_ref, qr_ref, cc_ref, ckr_ref, cn_ref, krn_ref, o_ref, *, n_new):
    rows = MLA_HEADS * n_new
    ql = ql_ref[...].reshape(rows, KV_LORA)
    qr = qr_ref[...].reshape(rows, QK_ROPE)
    cc = cc_ref[...].astype(BF16)
    cn = cn_ref[...].astype(BF16)
    s1 = (_dot_nt(ql, cc) + _dot_nt(qr, ckr_ref[...].astype(BF16))) * MLA_SCALE
    s2 = (_dot_nt(ql, cn) + _dot_nt(qr, krn_ref[...].astype(BF16))) * MLA_SCALE
    col = lax.broadcasted_iota(jnp.int32, s2.shape, 1)
    s2 = jnp.where(col < n_new, s2, NEG_BIG)
    m = jnp.maximum(jnp.max(s1, axis=-1, keepdims=True), jnp.max(s2, axis=-1, keepdims=True))
    p1 = jnp.exp(s1 - m)
    p2 = jnp.exp(s2 - m)
    l = jnp.sum(p1, axis=-1, keepdims=True) + jnp.sum(p2, axis=-1, keepdims=True)
    o = (jnp.dot(p1.astype(BF16), cc, preferred_element_type=F32)
         + jnp.dot(p2.astype(BF16), cn, preferred_element_type=F32))
    o_ref[...] = (o / l).astype(o_ref.dtype).reshape(MLA_HEADS, n_new, KV_LORA)


def _mla_decode_attn(qlat, qrope, cache_c, cache_kr, c_new, kr_new):
    nb, past, _ = cache_c.shape
    n_new = c_new.shape[1]
    per_chunk = CHUNK // n_new
    c_pad = jnp.pad(c_new, ((0, 0), (0, LANES - n_new), (0, 0)))
    kr_pad = jnp.pad(kr_new, ((0, 0), (0, LANES - n_new), (0, 0)))

    def qspec(width):
        return pl.BlockSpec((None, MLA_HEADS, n_new, width),
                            lambda b: (b // per_chunk, 0, b % per_chunk, 0))

    return pl.pallas_call(
        functools.partial(_mla_dec_body, n_new=n_new),
        grid=(nb,),
        in_specs=[qspec(KV_LORA), qspec(QK_ROPE),
                  pl.BlockSpec((None, past, KV_LORA), lambda b: (b, 0, 0)),
                  pl.BlockSpec((None, past, QK_ROPE), lambda b: (b, 0, 0)),
                  pl.BlockSpec((None, LANES, KV_LORA), lambda b: (b, 0, 0)),
                  pl.BlockSpec((None, LANES, QK_ROPE), lambda b: (b, 0, 0))],
        out_specs=qspec(KV_LORA),
        out_shape=jax.ShapeDtypeStruct(qlat.shape, BF16),
        compiler_params=_params(1),
        name="mla_decode_attn",
    )(qlat, qrope, cache_c, cache_kr, c_pad, kr_pad)


def _uv_body(o_ref, w_ref, out_ref, *, tm):
    for h in range(MLA_HEADS):
        a = jnp.concatenate([o_ref[j, h] for j in range(tm // CHUNK)], axis=0)
        out_ref[:, h * V_DIM:(h + 1) * V_DIM] = jnp.dot(
            a, w_ref[h].astype(BF16), preferred_element_type=F32).astype(out_ref.dtype)


def _mla_uv(o_lat, w_uv_t):
    m = o_lat.shape[0] * CHUNK
    tm = min(m, 512)
    return pl.pallas_call(
        functools.partial(_uv_body, tm=tm),
        grid=(m // tm,),
        in_specs=[pl.BlockSpec((tm // CHUNK, MLA_HEADS, CHUNK, KV_LORA), lambda i: (i, 0, 0, 0)),
                  pl.BlockSpec((MLA_HEADS, KV_LORA, V_DIM), lambda i: (0, 0, 0))],
        out_specs=pl.BlockSpec((tm, MLA_HEADS * V_DIM), lambda i: (i, 0)),
        out_shape=jax.ShapeDtypeStruct((m, MLA_HEADS * V_DIM), BF16),
        compiler_params=_params(1),
        name="mla_uv",
    )(o_lat, w_uv_t)


def _xattn_body(q_ref, k_ref, v_ref, o_ref, *, heads, hd):
    scale = hd ** -0.5
    for h in range(heads):
        sl = slice(h * hd, (h + 1) * hd)
        q = q_ref[:, sl].astype(BF16)
        k = k_ref[:, sl].astype(BF16)
        v = v_ref[:, sl].astype(BF16)
        s = _dot_nt(q, k) * scale
        e = jnp.exp(s - jnp.max(s, axis=-1, keepdims=True))
        p = e / jnp.sum(e, axis=-1, keepdims=True)
        o_ref[:, sl] = jnp.dot(p.astype(BF16), v, preferred_element_type=F32).astype(o_ref.dtype)


def _xattn(q, k, v):
    nb, t, d = q.shape
    n_mem = k.shape[1]
    tt = min(t, 512)
    return pl.pallas_call(
        functools.partial(_xattn_body, heads=XA_HEADS, hd=d // XA_HEADS),
        grid=(nb, t // tt),
        in_specs=[pl.BlockSpec((None, tt, d), lambda b, i: (b, i, 0)),
                  pl.BlockSpec((None, n_mem, d), lambda b, i: (b, 0, 0)),
                  pl.BlockSpec((None, n_mem, d), lambda b, i: (b, 0, 0))],
        out_specs=pl.BlockSpec((None, tt, d), lambda b, i: (b, i, 0)),
        out_shape=jax.ShapeDtypeStruct((nb, t, d), BF16),
        compiler_params=_params(2),
        name="xattn",
    )(q, k, v)


ROUTE_TN = LANES
STAT_ROWS = 8


def _sort_network(n):
    pairs = []
    p = 1
    while p < n:
        k = p
        while k >= 1:
            for j in range(k % p, n - k, 2 * k):
                for i in range(min(k, n - j - k)):
                    if (i + j) // (2 * p) == (i + j + k) // (2 * p):
                        pairs.append((i + j, i + j + k))
            k //= 2
        p *= 2
    return pairs


def _sorted_top(problems, n, rows_out):
    size = 1
    while size < max(len(p) for p in problems):
        size *= 2
    net = _sort_network(size)
    problems = [list(p) for p in problems]
    for i, j in net:
        for p in problems:
            if j < len(p):
                hi = jnp.maximum(p[i], p[j])
                p[j] = jnp.minimum(p[i], p[j])
                p[i] = hi
    outs = [[] for _ in problems]
    for r in range(n):
        for idx, p in enumerate(problems):
            p = p[:n - r]
            m = jnp.max(p[0], axis=0, keepdims=True)
            outs[idx].append(m)
            if r + 1 < n:
                hit = p[0] == m
                p = ([jnp.where(hit, p[k + 1], p[k]) for k in range(len(p) - 1)]
                     + [jnp.where(hit, -jnp.inf, p[-1])])
            problems[idx] = p
    lanes = problems[0][0].shape[1]
    pad = [jnp.full((rows_out - n, lanes), -jnp.inf, F32)]
    return [jnp.concatenate(o + pad, axis=0) for o in outs]


def _pieces(x):
    return [x[k:k + SUBLANES] for k in range(0, x.shape[0], SUBLANES)]


ROUTE_N = PEER_TOPK + 1
ROUTE_ROWS = 24


def _pair_sum_candidates(a, b):
    g = SUBLANES
    pieces = _pieces(a[0:1] + b)
    pieces += [a[r:r + 1] + b[0:g] for r in range(1, g)]
    pieces += _pieces(a[g:ROUTE_ROWS] + b[0:1])
    return pieces


def _route_body(q_ref, sk_ref, s0_ref, s1_ref, st_ref):
    sk0 = sk_ref[0].astype(BF16)
    sk1 = sk_ref[1].astype(BF16)
    half = sk_ref.shape[2]

    def head_scores(h):
        c0 = pl.multiple_of(h * 2 * half, LANES)
        c1 = pl.multiple_of(h * 2 * half + half, LANES)
        s0 = _dot_nt(sk0, q_ref[:, pl.ds(c0, half)].astype(BF16))
        s1 = _dot_nt(sk1, q_ref[:, pl.ds(c1, half)].astype(BF16))
        s0_ref[h] = s0
        s1_ref[h] = s1
        return _pieces(s0), _pieces(s1)

    def write_stats(h, a, b, cand, top):
        tau = 0.5 * (top[PEER_TOPK - 1:PEER_TOPK] + top[PEER_TOPK:PEER_TOPK + 1])
        zp = None
        for c in cand:
            t = jnp.where(c >= tau, jnp.exp(c - top[0:1]), 0.0)
            zp = t if zp is None else zp + t
        z = jnp.sum(zp, axis=0, keepdims=True)
        st_ref[h] = jnp.concatenate(
            [tau, a[0:1], b[0:1], 1.0 / z, jnp.zeros((STAT_ROWS - 4, ROUTE_TN), F32)], axis=0)

    def head_pair(hp, carry):
        h0 = hp * 2
        p00, p01 = head_scores(h0)
        p10, p11 = head_scores(h0 + 1)
        a0, b0, a1, b1 = _sorted_top([p00, p01, p10, p11], ROUTE_N, ROUTE_ROWS)
        cand0 = _pair_sum_candidates(a0, b0)
        cand1 = _pair_sum_candidates(a1, b1)
        top0, top1 = _sorted_top([cand0, cand1], ROUTE_N, ROUTE_ROWS)
        write_stats(h0, a0, b0, cand0, top0)
        write_stats(h0 + 1, a1, b1, cand1, top1)
        return carry

    lax.fori_loop(0, PEER_HEADS // 2, head_pair, 0)


def _peer_route(q, subkeys):
    n = q.shape[0]
    half = subkeys.shape[2]
    spec_s = pl.BlockSpec((PEER_HEADS, N_KEYS, ROUTE_TN), lambda i: (0, 0, i))
    return pl.pallas_call(
        _route_body,
        grid=(n // ROUTE_TN,),
        in_specs=[pl.BlockSpec((ROUTE_TN, q.shape[1]), lambda i: (i, 0)),
                  pl.BlockSpec((2, N_KEYS, half), lambda i: (0, 0, 0))],
        out_specs=[spec_s, spec_s,
                   pl.BlockSpec((PEER_HEADS, STAT_ROWS, ROUTE_TN), lambda i: (0, 0, i))],
        out_shape=[jax.ShapeDtypeStruct((PEER_HEADS, N_KEYS, n), F32),
                   jax.ShapeDtypeStruct((PEER_HEADS, N_KEYS, n), F32),
                   jax.ShapeDtypeStruct((PEER_HEADS, STAT_ROWS, n), F32)],
        compiler_params=_params(1),
        name="peer_route",
    )(q, subkeys)


PEER_EB = 1024


def _gelu_tanh(x):
    c = 0.7978845608028654
    t = jnp.tanh(x * (c + (c * 0.044715) * (x * x)))
    hx = 0.5 * x
    return hx + hx * t


def _peer_body(x_ref, g_ref, u_ref, v_ref, s0_ref, s1_ref, st_ref, *rest, tn, final_norm):
    fg_ref = rest[0] if final_norm else None
    o_ref, xn_ref, e1_ref, w_ref = rest[-4:]
    @pl.when(pl.program_id(1) == 0)
    def _():
        xn_ref[...] = _rms(x_ref[...], g_ref[...]).astype(BF16)
        for h in range(PEER_HEADS):
            e1_ref[h] = jnp.exp(s1_ref[h] - st_ref[2, h:h + 1, :])
        o_ref[...] = x_ref[...]

    act = _dot_nt(u_ref[...], xn_ref[...])
    for c in range(tn // LANES):
        lanes = slice(c * LANES, (c + 1) * LANES)
        for ii in range(PEER_EB // N_KEYS):
            s0rows = s0_ref[ii, :, lanes]
            thr_all = st_ref[0, :, lanes] - s0rows
            coef_all = jnp.exp(s0rows - st_ref[1, :, lanes]) * st_ref[3, :, lanes]
            g = None
            for h in range(PEER_HEADS):
                gh = jnp.where(s1_ref[h, :, lanes] >= thr_all[h:h + 1],
                               e1_ref[h, :, lanes] * coef_all[h:h + 1], 0.0)
                g = gh if g is None else g + gh
            a = act[ii * N_KEYS:(ii + 1) * N_KEYS, lanes]
            w_ref[ii * N_KEYS:(ii + 1) * N_KEYS, lanes] = (_gelu_tanh(a) * g).astype(BF16)
    o_ref[...] += lax.dot_general(w_ref[...], v_ref[...], (((0,), (0,)), ((), ())),
                                  preferred_element_type=F32)

    if fg_ref is not None:
        @pl.when(pl.program_id(1) == pl.num_programs(1) - 1)
        def _():
            o_ref[...] = _rms(o_ref[...], fg_ref[...])


def _peer_dense(x, g, u_tab, v_tab, s0t, s1t, stats, final_g=None):
    n, d = x.shape
    n_exp = u_tab.shape[0]
    tn = min(n, 512)
    groups = PEER_EB // N_KEYS
    s0_by_key = jnp.transpose(s0t, (1, 0, 2))
    stats_by_kind = jnp.transpose(stats, (1, 0, 2))
    extra_specs = [] if final_g is None else [pl.BlockSpec((1, d), lambda i, e: (0, 0))]
    extra_args = [] if final_g is None else [final_g.reshape(1, d)]
    return pl.pallas_call(
        functools.partial(_peer_body, tn=tn, final_norm=final_g is not None),
        grid=(n // tn, n_exp // PEER_EB),
        in_specs=[pl.BlockSpec((tn, d), lambda i, e: (i, 0)),
                  pl.BlockSpec((1, d), lambda i, e: (0, 0)),
                  pl.BlockSpec((PEER_EB, d), lambda i, e: (e, 0)),
                  pl.BlockSpec((PEER_EB, d), lambda i, e: (e, 0)),
                  pl.BlockSpec((groups, PEER_HEADS, tn), lambda i, e: (e, 0, i)),
                  pl.BlockSpec((PEER_HEADS, N_KEYS, tn), lambda i, e: (0, 0, i)),
                  pl.BlockSpec((STAT_ROWS, PEER_HEADS, tn), lambda i, e: (0, 0, i))] + extra_specs,
        out_specs=pl.BlockSpec((tn, d), lambda i, e: (i, 0)),
        out_shape=jax.ShapeDtypeStruct((n, d), F32),
        scratch_shapes=[pltpu.VMEM((tn, d), BF16),
                        pltpu.VMEM((PEER_HEADS, N_KEYS, tn), F32),
                        pltpu.VMEM((PEER_EB, tn), BF16)],
        compiler_params=_params(2),
        name="peer_dense",
    )(x, g.reshape(1, d), u_tab, v_tab, s0_by_key, s1t, stats_by_kind, *extra_args)


def _conv_layer(x, nb, t, ctx_a, ctx_b, g, w_in, ca_w, ca_b, ln_g, ln_b, cb_w, w_out):
    z = _norm_matmul(x, w_in, g, name="conv_in")
    y, new_a, new_b = _conv_mix(z.reshape(nb, t, -1), ctx_a, ctx_b, ca_w, ca_b, ln_g, ln_b, cb_w)
    x = _norm_matmul(y.reshape(nb * t, -1), w_out, res=x, name="conv_out")
    return x, new_a, new_b


def _mla_project(x, pos, g, w_down, q_norm, kv_norm, w_uq_perm, w_uk_t):
    z = _norm_matmul(x, w_down, g, name="mla_down")
    q = _norm_matmul(z, w_uq_perm, q_norm, name="mla_uq")
    qlat = _mla_qlat(q, w_uk_t)
    cos, sin = _rope_tables(pos)
    c32, cbf, kr32, krbf, qrope = _mla_post(z, q, cos, sin, kv_norm)
    return qlat, qrope, c32, cbf, kr32, krbf


def _mla_finish(x, o_lat, w_uv_t, w_o):
    return _norm_matmul(_mla_uv(o_lat, w_uv_t), w_o, res=x, name="mla_o")


def _xattn_layer(x, nb, t, k, v, g, wq, wo, layer):
    d = x.shape[1]
    q = _norm_matmul(x, wq, g, layer=layer, name="xa_q")
    o = _xattn(q.reshape(nb, t, d), k, v)
    return _norm_matmul(o.reshape(nb * t, d), wo, res=x, layer=layer, name="xa_o")


def _peer_layer(x, g, w_q, layer, subkeys, u_tab, v_tab, final_g=None):
    q = _norm_matmul(x, w_q, g, layer=layer, name="peer_q")
    s0t, s1t, stats = _peer_route(q, subkeys)
    return _peer_dense(x, g, u_tab, v_tab, s0t, s1t, stats, final_g)


def kernel(x_prompt, x_sample, cache_conv_a, cache_conv_b, cache_mla_latent, cache_mla_krope, cache_mem_k, cache_mem_v, mem_prompt, norm_mix, w_conv_in, conv_a_w, conv_a_b, ln_a_g, ln_a_b, conv_b_w, w_conv_out, w_mla_down, mla_q_norm, mla_kv_norm, w_mla_uq, w_mla_uk, w_mla_uv, w_mla_o, norm_xa, norm_mem, w_xa_q, w_xa_k, w_xa_v, w_xa_o, norm_ffn, peer_w_q, peer_subkeys, peer_u, peer_v, norm_final):
    bp, tp, d = x_prompt.shape
    bs, ts, _ = x_sample.shape
    depth = norm_mix.shape[0]
    past = cache_mla_latent.shape[2]
    n_mem = mem_prompt.shape[1]
    xp = x_prompt.reshape(bp * tp, d)
    xs = x_sample.reshape(bs * ts, d)
    pos_p = jnp.tile(jnp.arange(tp, dtype=jnp.int32), bp)
    pos_s = jnp.tile(past + jnp.arange(ts, dtype=jnp.int32), bs)
    mem2d = mem_prompt.reshape(bp * n_mem, d)

    w_conv_out, w_mla_o, w_xa_q, w_xa_k, w_xa_v, w_xa_o, peer_w_q = (
        w.astype(BF16) for w in (w_conv_out, w_mla_o, w_xa_q, w_xa_k, w_xa_v, w_xa_o, peer_w_q))

    conv_a_p, conv_b_p, lat_p, kr_p, mem_k_p, mem_v_p = [], [], [], [], [], []
    conv_a_s, conv_b_s, lat_s, kr_s = [], [], [], []
    for l in range(depth):
        j = l // 2
        if l % 2 == 0:
            d_a = conv_a_w.shape[-1]
            zero_a = jnp.zeros((bp, CONV_A_WIDTH - 1, d_a), F32)
            zero_b = jnp.zeros((bp, CONV_B_WIDTH - 1, conv_b_w.shape[-1]), F32)
            wts = (norm_mix[l], w_conv_in[j], conv_a_w[j], conv_a_b[j], ln_a_g[j], ln_a_b[j],
                   conv_b_w[j], w_conv_out[j])
            xp, sa, sb = _conv_layer(xp, bp, tp, zero_a, zero_b, *wts)
            conv_a_p.append(sa)
            conv_b_p.append(sb)
            xs, sa, sb = _conv_layer(xs, bs, ts, cache_conv_a[j], cache_conv_b[j], *wts)
            conv_a_s.append(sa)
            conv_b_s.append(sb)
        else:
            uq = w_mla_uq[j].reshape(Q_LORA, MLA_HEADS, QK_NOPE + QK_ROPE)
            w_uq_perm = jnp.concatenate(
                [uq[:, :, :QK_NOPE].reshape(Q_LORA, MLA_HEADS * QK_NOPE),
                 uq[:, :, QK_NOPE:].reshape(Q_LORA, MLA_HEADS * QK_ROPE)], axis=1)
            w_uk_t = jnp.transpose(w_mla_uk[j], (1, 2, 0))
            w_uv_t = jnp.transpose(w_mla_uv[j], (1, 0, 2))
            proj = (norm_mix[l], w_mla_down[j], mla_q_norm[j], mla_kv_norm[j], w_uq_perm, w_uk_t)

            ql, qr, c32, cbf, kr32, krbf = _mla_project(xp, pos_p, *proj)
            o_lat = _mla_prompt_attn(ql, qr, cbf.reshape(bp, tp, KV_LORA),
                                     krbf.reshape(bp, tp, QK_ROPE))
            xp = _mla_finish(xp, o_lat, w_uv_t, w_mla_o[j])
            lat_p.append(c32.reshape(bp, tp, KV_LORA))
            kr_p.append(kr32.reshape(bp, tp, QK_ROPE))

            ql, qr, c32, cbf, kr32, krbf = _mla_project(xs, pos_s, *proj)
            c_new = c32.reshape(bs, ts, KV_LORA)
            kr_new = kr32.reshape(bs, ts, QK_ROPE)
            o_lat = _mla_decode_attn(ql, qr, cache_mla_latent[j], cache_mla_krope[j], c_new, kr_new)
            xs = _mla_finish(xs, o_lat, w_uv_t, w_mla_o[j])
            lat_s.append(c_new)
            kr_s.append(kr_new)

        mk = _norm_matmul(mem2d, w_xa_k, norm_mem[l], layer=l, name="mem_k")
        mv = _norm_matmul(mem2d, w_xa_v, norm_mem[l], layer=l, name="mem_v")
        hd = d // XA_HEADS
        mem_k_p.append(mk.reshape(bp, n_mem, XA_HEADS, hd))
        mem_v_p.append(mv.reshape(bp, n_mem, XA_HEADS, hd))
        xp = _xattn_layer(xp, bp, tp, mk.reshape(bp, n_mem, d), mv.reshape(bp, n_mem, d),
                          norm_xa[l], w_xa_q, w_xa_o, l)
        xs = _xattn_layer(xs, bs, ts, cache_mem_k[l].reshape(bs, n_mem, d),
                          cache_mem_v[l].reshape(bs, n_mem, d), norm_xa[l], w_xa_q, w_xa_o, l)

        u_tab = peer_u[l].astype(BF16)
        v_tab = peer_v[l].astype(BF16)
        final_g = norm_final if l == depth - 1 else None
        xp = _peer_layer(xp, norm_ffn[l], peer_w_q, l, peer_subkeys[l], u_tab, v_tab, final_g)
        xs = _peer_layer(xs, norm_ffn[l], peer_w_q, l, peer_subkeys[l], u_tab, v_tab, final_g)

    y_prompt = xp.reshape(bp, tp, d)
    y_sample = xs.reshape(bs, ts, d)
    return (y_prompt, y_sample,
            jnp.stack(conv_a_p), jnp.stack(conv_b_p), jnp.stack(lat_p), jnp.stack(kr_p),
            jnp.stack(mem_k_p), jnp.stack(mem_v_p),
            jnp.stack(conv_a_s), jnp.stack(conv_b_s), jnp.stack(lat_s), jnp.stack(kr_s))
```

```python
import functools

import jax
import jax.numpy as jnp
from jax import lax
from jax.experimental import pallas as pl
from jax.experimental.pallas import tpu as pltpu

F32 = jnp.float32
BF16 = jnp.bfloat16

EPS = 1e-6
CHUNK = 64
CONV_A_WIDTH = 31
CONV_B_WIDTH = 3
MLA_HEADS = 16
Q_LORA = 512
KV_LORA = 512
QK_NOPE = 128
QK_ROPE = 64
V_DIM = 128
ROPE_THETA = 10000.0
MLA_SCALE = (QK_NOPE + QK_ROPE) ** -0.5
XA_HEADS = 4
PEER_HEADS = 8
N_KEYS = 128
PEER_TOPK = 16

LANES = 128
SUBLANES = 8
NEG_BIG = -1e30
MIB = 1024 * 1024
VMEM_LIMIT = 56 * MIB
RESIDENT_W_BYTES = 8 * MIB
SINGLE_BUFFER_W_BYTES = 20 * MIB


def _params(n_axes, vmem=VMEM_LIMIT):
    return pltpu.CompilerParams(dimension_semantics=("arbitrary",) * n_axes,
                                vmem_limit_bytes=vmem)


def _dot_nt(a, b):
    return lax.dot_general(a, b, (((1,), (1,)), ((), ())), preferred_element_type=F32)


def _rms(x, g):
    return x * lax.rsqrt(jnp.mean(x * x, axis=-1, keepdims=True) + EPS) * g


def _nm_body(*refs, has_norm, has_res):
    it = iter(refs)
    x_ref = next(it)
    g_ref = next(it) if has_norm else None
    w_ref = next(it)
    r_ref = next(it) if has_res else None
    o_ref = next(it)
    xn_ref = next(it)

    @pl.when(pl.program_id(1) == 0)
    def _():
        x = x_ref[...].astype(F32)
        if has_norm:
            x = _rms(x, g_ref[...])
        xn_ref[...] = x.astype(BF16)

    acc = jnp.dot(xn_ref[...], w_ref[...].astype(BF16), preferred_element_type=F32)
    if has_res:
        acc = acc + r_ref[...]
    o_ref[...] = acc.astype(o_ref.dtype)


def _norm_matmul(x, w, g=None, res=None, *, layer=None, x_col_block=0, out_dtype=F32, name):
    m = x.shape[0]
    k, n = w.shape[-2:]
    w_mode = {}
    if k * n * w.dtype.itemsize <= RESIDENT_W_BYTES:
        tm, tn = min(m, 512), n
    elif k * n * w.dtype.itemsize <= SINGLE_BUFFER_W_BYTES:
        tm, tn = min(m, 512), n
        w_mode = {"pipeline_mode": pl.Buffered(1)}
    else:
        tm, tn = min(m, 1024), (512 if n % 512 == 0 else n)
    has_norm = g is not None
    has_res = res is not None
    in_specs = [pl.BlockSpec((tm, k), lambda i, j: (i, x_col_block))]
    args = [x]
    if has_norm:
        in_specs.append(pl.BlockSpec((1, k), lambda i, j: (0, 0)))
        args.append(g.reshape(1, k))
    if layer is None:
        in_specs.append(pl.BlockSpec((k, tn), lambda i, j: (0, j), **w_mode))
    else:
        in_specs.append(pl.BlockSpec((None, k, tn), lambda i, j: (layer, 0, j), **w_mode))
    args.append(w)
    if has_res:
        in_specs.append(pl.BlockSpec((tm, tn), lambda i, j: (i, j)))
        args.append(res)
    return pl.pallas_call(
        functools.partial(_nm_body, has_norm=has_norm, has_res=has_res),
        grid=(m // tm, n // tn),
        in_specs=in_specs,
        out_specs=pl.BlockSpec((tm, tn), lambda i, j: (i, j)),
        out_shape=jax.ShapeDtypeStruct((m, n), out_dtype),
        scratch_shapes=[pltpu.VMEM((tm, k), BF16)],
        compiler_params=_params(2),
        name=name,
    )(*args)


A_HIST = 32
B_HIST = 8


def _conv_body(av_ref, ag_ref, bg_ref, cg_ref, v_ref, ctxa_ref, ctxb_ref, caw_ref, cab_ref,
               lng_ref, lnb_ref, cbw_ref, y_ref, ta_ref, tb_ref, sa_ref, sb_ref, *, tt, d_a):
    t = pl.program_id(1)

    @pl.when(t == 0)
    def _():
        sa_ref[0:A_HIST, :] = ctxa_ref[...]
        sb_ref[0:B_HIST, :] = ctxb_ref[...]

    @pl.when(t > 0)
    def _():
        sa_ref[0:A_HIST, :] = sa_ref[tt:tt + A_HIST, :]
        sb_ref[0:B_HIST, :] = sb_ref[tt:tt + B_HIST, :]

    sa_ref[A_HIST:A_HIST + tt, :] = av_ref[...] * jax.nn.sigmoid(ag_ref[...])
    sb_ref[B_HIST:B_HIST + tt, :] = cg_ref[...] * v_ref[...]

    off_a = A_HIST - (CONV_A_WIDTH - 1)
    acc = jnp.broadcast_to(cab_ref[...], (tt, d_a))
    for k in range(CONV_A_WIDTH):
        acc = acc + caw_ref[k:k + 1, :] * sa_ref[off_a + k:off_a + k + tt, :]
    mu = jnp.mean(acc, axis=-1, keepdims=True)
    xc = acc - mu
    var = jnp.mean(xc * xc, axis=-1, keepdims=True)
    ln = xc * lax.rsqrt(var + EPS) * lng_ref[...] + lnb_ref[...]
    y_ref[:, 0:d_a] = (ln * jax.nn.sigmoid(ln)).astype(y_ref.dtype)

    off_b = B_HIST - (CONV_B_WIDTH - 1)
    accb = cbw_ref[0:1, :] * sb_ref[off_b:off_b + tt, :]
    for k in range(1, CONV_B_WIDTH):
        accb = accb + cbw_ref[k:k + 1, :] * sb_ref[off_b + k:off_b + k + tt, :]
    y_ref[:, d_a:] = (bg_ref[...] * accb).astype(y_ref.dtype)

    ta_ref[...] = sa_ref[tt:tt + A_HIST, :]
    tb_ref[...] = sb_ref[tt:tt + B_HIST, :]


def _conv_mix(z, ctx_a, ctx_b, ca_w, ca_b, ln_g, ln_b, cb_w):
    nb, t, _ = z.shape
    d_a = ca_w.shape[-1]
    tt = min(t, 256)
    ctx_a = jnp.pad(ctx_a, ((0, 0), (A_HIST - ctx_a.shape[1], 0), (0, 0)))
    ctx_b = jnp.pad(ctx_b, ((0, 0), (B_HIST - ctx_b.shape[1], 0), (0, 0)))
    caw = jnp.pad(ca_w, ((0, A_HIST - CONV_A_WIDTH), (0, 0)))
    cbw = jnp.pad(cb_w, ((0, B_HIST - CONV_B_WIDTH), (0, 0)))

    def zspec(col):
        return pl.BlockSpec((None, tt, d_a), lambda b, i: (b, i, col))

    def vec(rows):
        return pl.BlockSpec((rows, d_a), lambda b, i: (0, 0))

    y, ta, tb = pl.pallas_call(
        functools.partial(_conv_body, tt=tt, d_a=d_a),
        grid=(nb, t // tt),
        in_specs=[zspec(0), zspec(1), zspec(2), zspec(3), zspec(4),
                  pl.BlockSpec((None, A_HIST, d_a), lambda b, i: (b, 0, 0)),
                  pl.BlockSpec((None, B_HIST, d_a), lambda b, i: (b, 0, 0)),
                  vec(A_HIST), vec(1), vec(1), vec(1), vec(B_HIST)],
        out_specs=[pl.BlockSpec((None, tt, 2 * d_a), lambda b, i: (b, i, 0)),
                   pl.BlockSpec((None, A_HIST, d_a), lambda b, i: (b, 0, 0)),
                   pl.BlockSpec((None, B_HIST, d_a), lambda b, i: (b, 0, 0))],
        out_shape=[jax.ShapeDtypeStruct((nb, t, 2 * d_a), BF16),
                   jax.ShapeDtypeStruct((nb, A_HIST, d_a), F32),
                   jax.ShapeDtypeStruct((nb, B_HIST, d_a), F32)],
        scratch_shapes=[pltpu.VMEM((tt + A_HIST, d_a), F32),
                        pltpu.VMEM((tt + B_HIST, d_a), F32)],
        compiler_params=_params(2),
        name="conv_mix",
    )(z, z, z, z, z, ctx_a, ctx_b, caw, ca_b.reshape(1, d_a), ln_g.reshape(1, d_a),
      ln_b.reshape(1, d_a), cbw)
    return (y, ta[:, A_HIST - (CONV_A_WIDTH - 1):], tb[:, B_HIST - (CONV_B_WIDTH - 1):])


def _qlat_body(q_ref, w_ref, o_ref, *, tm):
    for h in range(MLA_HEADS):
        a = q_ref[:, h * QK_NOPE:(h + 1) * QK_NOPE].astype(BF16)
        r = jnp.dot(a, w_ref[h].astype(BF16), preferred_element_type=F32).astype(BF16)
        for j in range(tm // CHUNK):
            o_ref[j, h] = r[j * CHUNK:(j + 1) * CHUNK]


def _mla_qlat(q, w_uk_t):
    m = q.shape[0]
    tm = min(m, 512)
    return pl.pallas_call(
        functools.partial(_qlat_body, tm=tm),
        grid=(m // tm,),
        in_specs=[pl.BlockSpec((tm, MLA_HEADS * QK_NOPE), lambda i: (i, 0)),
                  pl.BlockSpec((MLA_HEADS, QK_NOPE, KV_LORA), lambda i: (0, 0, 0))],
        out_specs=pl.BlockSpec((tm // CHUNK, MLA_HEADS, CHUNK, KV_LORA), lambda i: (i, 0, 0, 0)),
        out_shape=jax.ShapeDtypeStruct((m // CHUNK, MLA_HEADS, CHUNK, KV_LORA), BF16),
        compiler_params=_params(1),
        name="mla_qlat",
    )(q, w_uk_t)


def _mla_post_body(z_ref, qr_ref, cos_ref, sin_ref, g_ref, c32_ref, cbf_ref, kr32_ref, krbf_ref,
                   qro_ref, *, tm):
    z = z_ref[...]
    c = _rms(z[:, Q_LORA:Q_LORA + KV_LORA], g_ref[...])
    c32_ref[...] = c
    cbf_ref[...] = c.astype(BF16)

    cos = cos_ref[...]
    sin = sin_ref[...]
    kr = z[:, Q_LORA + KV_LORA:]
    half = QK_ROPE // 2
    kr_sw = jnp.concatenate([kr[:, half:], kr[:, :half]], axis=1)
    kr_rot = kr * cos[:, :QK_ROPE] + kr_sw * sin[:, :QK_ROPE]
    kr32_ref[...] = kr_rot
    krbf_ref[...] = kr_rot.astype(BF16)

    q = qr_ref[...]
    width = q.shape[1]
    lane = lax.broadcasted_iota(jnp.int32, q.shape, 1)
    first_half = (lane % QK_ROPE) < half
    q_sw = jnp.where(first_half, pltpu.roll(q, width - half, axis=1), pltpu.roll(q, half, axis=1))
    reps = width // LANES
    cos_t = jnp.concatenate([cos] * reps, axis=1)
    sin_t = jnp.concatenate([sin] * reps, axis=1)
    q_rot = q * cos_t + q_sw * sin_t
    for h in range(MLA_HEADS):
        piece = q_rot[:, h * QK_ROPE:(h + 1) * QK_ROPE].astype(BF16)
        for j in range(tm // CHUNK):
            qro_ref[j, h] = piece[j * CHUNK:(j + 1) * CHUNK]


def _mla_post(z, q, cos, sin, kv_norm):
    m = z.shape[0]
    tm = min(m, 512)
    rope_w = MLA_HEADS * QK_ROPE
    rope_block = (MLA_HEADS * QK_NOPE) // rope_w
    return pl.pallas_call(
        functools.partial(_mla_post_body, tm=tm),
        grid=(m // tm,),
        in_specs=[pl.BlockSpec((tm, z.shape[1]), lambda i: (i, 0)),
                  pl.BlockSpec((tm, rope_w), lambda i: (i, rope_block)),
                  pl.BlockSpec((tm, LANES), lambda i: (i, 0)),
                  pl.BlockSpec((tm, LANES), lambda i: (i, 0)),
                  pl.BlockSpec((1, KV_LORA), lambda i: (0, 0))],
        out_specs=[pl.BlockSpec((tm, KV_LORA), lambda i: (i, 0)),
                   pl.BlockSpec((tm, KV_LORA), lambda i: (i, 0)),
                   pl.BlockSpec((tm, QK_ROPE), lambda i: (i, 0)),
                   pl.BlockSpec((tm, QK_ROPE), lambda i: (i, 0)),
                   pl.BlockSpec((tm // CHUNK, MLA_HEADS, CHUNK, QK_ROPE), lambda i: (i, 0, 0, 0))],
        out_shape=[jax.ShapeDtypeStruct((m, KV_LORA), F32),
                   jax.ShapeDtypeStruct((m, KV_LORA), BF16),
                   jax.ShapeDtypeStruct((m, QK_ROPE), F32),
                   jax.ShapeDtypeStruct((m, QK_ROPE), BF16),
                   jax.ShapeDtypeStruct((m // CHUNK, MLA_HEADS, CHUNK, QK_ROPE), BF16)],
        compiler_params=_params(1),
        name="mla_post",
    )(z, q, cos, sin, kv_norm.reshape(1, KV_LORA))


def _rope_tables(pos):
    inv = 1.0 / (ROPE_THETA ** (jnp.arange(0, QK_ROPE, 2, dtype=F32) / QK_ROPE))
    ang = pos.astype(F32)[:, None] * inv[None, :]
    cos, sin = jnp.cos(ang), jnp.sin(ang)
    return (jnp.concatenate([cos, cos, cos, cos], axis=1),
            jnp.concatenate([-sin, sin, -sin, sin], axis=1))


MLA_KB = 512


def _mla_attn_body(ql_ref, qr_ref, c_ref, kr_ref, o_ref, s_ref, m_ref, l_ref, acc_ref):
    ci = pl.program_id(1)
    rows = MLA_HEADS * CHUNK
    ql = ql_ref[...].reshape(rows, KV_LORA)
    qr = qr_ref[...].reshape(rows, QK_ROPE)
    n_valid = (ci + 1) * CHUNK
    n_full = n_valid // MLA_KB
    tiles = MLA_KB // LANES
    m_ref[...] = jnp.full((rows, LANES), NEG_BIG, F32)

    def scores(kb, n_keep):
        k0 = pl.multiple_of(kb * MLA_KB, MLA_KB)
        s = (_dot_nt(ql, c_ref[pl.ds(k0, MLA_KB), :])
             + _dot_nt(qr, kr_ref[pl.ds(k0, MLA_KB), :])) * MLA_SCALE
        if n_keep is not None:
            col = lax.broadcasted_iota(jnp.int32, s.shape, 1)
            s = jnp.where(col < n_keep, s, NEG_BIG)
        s_ref[kb] = s
        mp = m_ref[...]
        for t in range(tiles):
            mp = jnp.maximum(mp, s[:, t * LANES:(t + 1) * LANES])
        m_ref[...] = mp

    def pass1(kb, carry):
        scores(kb, None)
        return carry

    lax.fori_loop(0, n_full, pass1, 0)
    rem = n_valid - n_full * MLA_KB

    @pl.when(rem > 0)
    def _():
        scores(n_full, rem)

    m = jnp.max(m_ref[...], axis=-1, keepdims=True)
    m_ref[...] = jnp.broadcast_to(m, (rows, LANES))
    l_ref[...] = jnp.zeros((rows, LANES), F32)
    acc_ref[...] = jnp.zeros((rows, KV_LORA), F32)

    def pass2(kb, carry):
        k0 = pl.multiple_of(kb * MLA_KB, MLA_KB)
        m_rep = m_ref[...]
        p = jnp.exp(s_ref[kb] - jnp.concatenate([m_rep] * tiles, axis=1))
        lp = l_ref[...]
        for t in range(tiles):
            lp = lp + p[:, t * LANES:(t + 1) * LANES]
        l_ref[...] = lp
        acc_ref[...] += jnp.dot(p.astype(BF16), c_ref[pl.ds(k0, MLA_KB), :],
                                preferred_element_type=F32)
        return carry

    lax.fori_loop(0, (n_valid + MLA_KB - 1) // MLA_KB, pass2, 0)
    l = jnp.sum(l_ref[...], axis=-1, keepdims=True)
    o_ref[...] = (acc_ref[...] / l).astype(o_ref.dtype).reshape(MLA_HEADS, CHUNK, KV_LORA)


def _mla_prompt_attn(qlat, qrope, cbf, krbf):
    nb, t, _ = cbf.shape
    nc = t // CHUNK
    rows = MLA_HEADS * CHUNK
    return pl.pallas_call(
        _mla_attn_body,
        grid=(nb, nc),
        in_specs=[pl.BlockSpec((None, MLA_HEADS, CHUNK, KV_LORA), lambda b, i: (b * nc + i, 0, 0, 0)),
                  pl.BlockSpec((None, MLA_HEADS, CHUNK, QK_ROPE), lambda b, i: (b * nc + i, 0, 0, 0)),
                  pl.BlockSpec((None, t, KV_LORA), lambda b, i: (b, 0, 0)),
                  pl.BlockSpec((None, t, QK_ROPE), lambda b, i: (b, 0, 0))],
        out_specs=pl.BlockSpec((None, MLA_HEADS, CHUNK, KV_LORA), lambda b, i: (b * nc + i, 0, 0, 0)),
        out_shape=jax.ShapeDtypeStruct(qlat.shape, BF16),
        scratch_shapes=[pltpu.VMEM((t // MLA_KB, rows, MLA_KB), F32),
                        pltpu.VMEM((rows, LANES), F32), pltpu.VMEM((rows, LANES), F32),
                        pltpu.VMEM((rows, KV_LORA), F32)],
        compiler_params=_params(2),
        name="mla_prompt_attn",
    )(qlat, qrope, cbf, krbf)


def _mla_dec_body(ql_ref, qr_ref, cc_ref, ckr_ref, cn_ref, krn_ref, o_ref, *, n_new):
    rows = MLA_HEADS * n_new
    ql = ql_ref[...].reshape(rows, KV_LORA)
    qr = qr_ref[...].reshape(rows, QK_ROPE)
    cc = cc_ref[...].astype(BF16)
    cn = cn_ref[...].astype(BF16)
    s1 = (_dot_nt(ql, cc) + _dot_nt(qr, ckr_ref[...].astype(BF16))) * MLA_SCALE
    s2 = (_dot_nt(ql, cn) + _dot_nt(qr, krn_ref[...].astype(BF16))) * MLA_SCALE
    col = lax.broadcasted_iota(jnp.int32, s2.shape, 1)
    s2 = jnp.where(col < n_new, s2, NEG_BIG)
    m = jnp.maximum(jnp.max(s1, axis=-1, keepdims=True), jnp.max(s2, axis=-1, keepdims=True))
    p1 = jnp.exp(s1 - m)
    p2 = jnp.exp(s2 - m)
    l = jnp.sum(p1, axis=-1, keepdims=True) + jnp.sum(p2, axis=-1, keepdims=True)
    o = (jnp.dot(p1.astype(BF16), cc, preferred_element_type=F32)
         + jnp.dot(p2.astype(BF16), cn, preferred_element_type=F32))
    o_ref[...] = (o / l).astype(o_ref.dtype).reshape(MLA_HEADS, n_new, KV_LORA)


def _mla_decode_attn(qlat, qrope, cache_c, cache_kr, c_new, kr_new):
    nb, past, _ = cache_c.shape
    n_new = c_new.shape[1]
    per_chunk = CHUNK // n_new
    c_pad = jnp.pad(c_new, ((0, 0), (0, LANES - n_new), (0, 0)))
    kr_pad = jnp.pad(kr_new, ((0, 0), (0, LANES - n_new), (0, 0)))

    def qspec(width):
        return pl.BlockSpec((None, MLA_HEADS, n_new, width),
                            lambda b: (b // per_chunk, 0, b % per_chunk, 0))

    return pl.pallas_call(
        functools.partial(_mla_dec_body, n_new=n_new),
        grid=(nb,),
        in_specs=[qspec(KV_LORA), qspec(QK_ROPE),
                  pl.BlockSpec((None, past, KV_LORA), lambda b: (b, 0, 0)),
                  pl.BlockSpec((None, past, QK_ROPE), lambda b: (b, 0, 0)),
                  pl.BlockSpec((None, LANES, KV_LORA), lambda b: (b, 0, 0)),
                  pl.BlockSpec((None, LANES, QK_ROPE), lambda b: (b, 0, 0))],
        out_specs=qspec(KV_LORA),
        out_shape=jax.ShapeDtypeStruct(qlat.shape, BF16),
        compiler_params=_params(1),
        name="mla_decode_attn",
    )(qlat, qrope, cache_c, cache_kr, c_pad, kr_pad)


def _uv_body(o_ref, w_ref, out_ref, *, tm):
    for h in range(MLA_HEADS):
        a = jnp.concatenate([o_ref[j, h] for j in range(tm // CHUNK)], axis=0)
        out_ref[:, h * V_DIM:(h + 1) * V_DIM] = jnp.dot(
            a, w_ref[h].astype(BF16), preferred_element_type=F32).astype(out_ref.dtype)


def _mla_uv(o_lat, w_uv_t):
    m = o_lat.shape[0] * CHUNK
    tm = min(m, 512)
    return pl.pallas_call(
        functools.partial(_uv_body, tm=tm),
        grid=(m // tm,),
        in_specs=[pl.BlockSpec((tm // CHUNK, MLA_HEADS, CHUNK, KV_LORA), lambda i: (i, 0, 0, 0)),
                  pl.BlockSpec((MLA_HEADS, KV_LORA, V_DIM), lambda i: (0, 0, 0))],
        out_specs=pl.BlockSpec((tm, MLA_HEADS * V_DIM), lambda i: (i, 0)),
        out_shape=jax.ShapeDtypeStruct((m, MLA_HEADS * V_DIM), BF16),
        compiler_params=_params(1),
        name="mla_uv",
    )(o_lat, w_uv_t)


def _xattn_body(q_ref, k_ref, v_ref, o_ref, *, heads, hd):
    scale = hd ** -0.5
    for h in range(heads):
        sl = slice(h * hd, (h + 1) * hd)
        q = q_ref[:, sl].astype(BF16)
        k = k_ref[:, sl].astype(BF16)
        v = v_ref[:, sl].astype(BF16)
        s = _dot_nt(q, k) * scale
        e = jnp.exp(s - jnp.max(s, axis=-1, keepdims=True))
        p = e / jnp.sum(e, axis=-1, keepdims=True)
        o_ref[:, sl] = jnp.dot(p.astype(BF16), v, preferred_element_type=F32).astype(o_ref.dtype)


def _xattn(q, k, v):
    nb, t, d = q.shape
    n_mem = k.shape[1]
    tt = min(t, 512)
    return pl.pallas_call(
        functools.partial(_xattn_body, heads=XA_HEADS, hd=d // XA_HEADS),
        grid=(nb, t // tt),
        in_specs=[pl.BlockSpec((None, tt, d), lambda b, i: (b, i, 0)),
                  pl.BlockSpec((None, n_mem, d), lambda b, i: (b, 0, 0)),
                  pl.BlockSpec((None, n_mem, d), lambda b, i: (b, 0, 0))],
        out_specs=pl.BlockSpec((None, tt, d), lambda b, i: (b, i, 0)),
        out_shape=jax.ShapeDtypeStruct((nb, t, d), BF16),
        compiler_params=_params(2),
        name="xattn",
    )(q, k, v)


ROUTE_TN = LANES
STAT_ROWS = 8


def _sort_network(n):
    pairs = []
    p = 1
    while p < n:
        k = p
        while k >= 1:
            for j in range(k % p, n - k, 2 * k):
                for i in range(min(k, n - j - k)):
                    if (i + j) // (2 * p) == (i + j + k) // (2 * p):
                        pairs.append((i + j, i + j + k))
            k //= 2
        p *= 2
    return pairs


def _sorted_top(problems, n, rows_out):
    size = 1
    while size < max(len(p) for p in problems):
        size *= 2
    net = _sort_network(size)
    problems = [list(p) for p in problems]
    for i, j in net:
        for p in problems:
            if j < len(p):
                hi = jnp.maximum(p[i], p[j])
                p[j] = jnp.minimum(p[i], p[j])
                p[i] = hi
    outs = [[] for _ in problems]
    for r in range(n):
        for idx, p in enumerate(problems):
            p = p[:n - r]
            m = jnp.max(p[0], axis=0, keepdims=True)
            outs[idx].append(m)
            if r + 1 < n:
                hit = p[0] == m
                p = ([jnp.where(hit, p[k + 1], p[k]) for k in range(len(p) - 1)]
                     + [jnp.where(hit, -jnp.inf, p[-1])])
            problems[idx] = p
    lanes = problems[0][0].shape[1]
    pad = [jnp.full((rows_out - n, lanes), -jnp.inf, F32)]
    return [jnp.concatenate(o + pad, axis=0) for o in outs]


def _pieces(x):
    return [x[k:k + SUBLANES] for k in range(0, x.shape[0], SUBLANES)]


ROUTE_N = PEER_TOPK + 1
ROUTE_ROWS = 24


def _pair_sum_candidates(a, b):
    g = SUBLANES
    pieces = _pieces(a[0:1] + b)
    pieces += [a[r:r + 1] + b[0:g] for r in range(1, g)]
    pieces += _pieces(a[g:ROUTE_ROWS] + b[0:1])
    return pieces


def _route_body(q_ref, sk_ref, s0_ref, s1_ref, st_ref):
    sk0 = sk_ref[0].astype(BF16)
    sk1 = sk_ref[1].astype(BF16)
    half = sk_ref.shape[2]

    def head_scores(h):
        c0 = pl.multiple_of(h * 2 * half, LANES)
        c1 = pl.multiple_of(h * 2 * half + half, LANES)
        s0 = _dot_nt(sk0, q_ref[:, pl.ds(c0, half)].astype(BF16))
        s1 = _dot_nt(sk1, q_ref[:, pl.ds(c1, half)].astype(BF16))
        s0_ref[h] = s0
        s1_ref[h] = s1
        return _pieces(s0), _pieces(s1)

    def write_stats(h, a, b, cand, top):
        tau = 0.5 * (top[PEER_TOPK - 1:PEER_TOPK] + top[PEER_TOPK:PEER_TOPK + 1])
        zp = None
        for c in cand:
            t = jnp.where(c >= tau, jnp.exp(c - top[0:1]), 0.0)
            zp = t if zp is None else zp + t
        z = jnp.sum(zp, axis=0, keepdims=True)
        st_ref[h] = jnp.concatenate(
            [tau, a[0:1], b[0:1], 1.0 / z, jnp.zeros((STAT_ROWS - 4, ROUTE_TN), F32)], axis=0)

    def head_pair(hp, carry):
        h0 = hp * 2
        p00, p01 = head_scores(h0)
        p10, p11 = head_scores(h0 + 1)
        a0, b0, a1, b1 = _sorted_top([p00, p01, p10, p11], ROUTE_N, ROUTE_ROWS)
        cand0 = _pair_sum_candidates(a0, b0)
        cand1 = _pair_sum_candidates(a1, b1)
        top0, top1 = _sorted_top([cand0, cand1], ROUTE_N, ROUTE_ROWS)
        write_stats(h0, a0, b0, cand0, top0)
        write_stats(h0 + 1, a1, b1, cand1, top1)
        return carry

    lax.fori_loop(0, PEER_HEADS // 2, head_pair, 0)


def _peer_route(q, subkeys):
    n = q.shape[0]
    half = subkeys.shape[2]
    spec_s = pl.BlockSpec((PEER_HEADS, N_KEYS, ROUTE_TN), lambda i: (0, 0, i))
    return pl.pallas_call(
        _route_body,
        grid=(n // ROUTE_TN,),
        in_specs=[pl.BlockSpec((ROUTE_TN, q.shape[1]), lambda i: (i, 0)),
                  pl.BlockSpec((2, N_KEYS, half), lambda i: (0, 0, 0))],
        out_specs=[spec_s, spec_s,
                   pl.BlockSpec((PEER_HEADS, STAT_ROWS, ROUTE_TN), lambda i: (0, 0, i))],
        out_shape=[jax.ShapeDtypeStruct((PEER_HEADS, N_KEYS, n), F32),
                   jax.ShapeDtypeStruct((PEER_HEADS, N_KEYS, n), F32),
                   jax.ShapeDtypeStruct((PEER_HEADS, STAT_ROWS, n), F32)],
        compiler_params=_params(1),
        name="peer_route",
    )(q, subkeys)


PEER_EB = 1024


def _gelu_tanh(x):
    c = 0.7978845608028654
    t = jnp.tanh(x * (c + (c * 0.044715) * (x * x)))
    hx = 0.5 * x
    return hx + hx * t


def _peer_body(x_ref, g_ref, u_ref, v_ref, s0_ref, s1_ref, st_ref, *rest, tn, final_norm):
    fg_ref = rest[0] if final_norm else None
    o_ref, xn_ref, e1_ref, w_ref = rest[-4:]
    @pl.when(pl.program_id(1) == 0)
    def _():
        xn_ref[...] = _rms(x_ref[...], g_ref[...]).astype(BF16)
        for h in range(PEER_HEADS):
            e1_ref[h] = jnp.exp(s1_ref[h] - st_ref[2, h:h + 1, :])
        o_ref[...] = x_ref[...]

    act = _dot_nt(u_ref[...], xn_ref[...])
    for c in range(tn // LANES):
        lanes = slice(c * LANES, (c + 1) * LANES)
        for ii in range(PEER_EB // N_KEYS):
            s0rows = s0_ref[ii, :, lanes]
            thr_all = st_ref[0, :, lanes] - s0rows
            coef_all = jnp.exp(s0rows - st_ref[1, :, lanes]) * st_ref[3, :, lanes]
            g = None
            for h in range(PEER_HEADS):
                gh = jnp.where(s1_ref[h, :, lanes] >= thr_all[h:h + 1],
                               e1_ref[h, :, lanes] * coef_all[h:h + 1], 0.0)
                g = gh if g is None else g + gh
            a = act[ii * N_KEYS:(ii + 1) * N_KEYS, lanes]
            w_ref[ii * N_KEYS:(ii + 1) * N_KEYS, lanes] = (_gelu_tanh(a) * g).astype(BF16)
    o_ref[...] += lax.dot_general(w_ref[...], v_ref[...], (((0,), (0,)), ((), ())),
                                  preferred_element_type=F32)

    if fg_ref is not None:
        @pl.when(pl.program_id(1) == pl.num_programs(1) - 1)
        def _():
            o_ref[...] = _rms(o_ref[...], fg_ref[...])


def _peer_dense(x, g, u_tab, v_tab, s0t, s1t, stats, final_g=None):
    n, d = x.shape
    n_exp = u_tab.shape[0]
    tn = min(n, 512)
    groups = PEER_EB // N_KEYS
    s0_by_key = jnp.transpose(s0t, (1, 0, 2))
    stats_by_kind = jnp.transpose(stats, (1, 0, 2))
    extra_specs = [] if final_g is None else [pl.BlockSpec((1, d), lambda i, e: (0, 0))]
    extra_args = [] if final_g is None else [final_g.reshape(1, d)]
    return pl.pallas_call(
        functools.partial(_peer_body, tn=tn, final_norm=final_g is not None),
        grid=(n // tn, n_exp // PEER_EB),
        in_specs=[pl.BlockSpec((tn, d), lambda i, e: (i, 0)),
                  pl.BlockSpec((1, d), lambda i, e: (0, 0)),
                  pl.BlockSpec((PEER_EB, d), lambda i, e: (e, 0)),
                  pl.BlockSpec((PEER_EB, d), lambda i, e: (e, 0)),
                  pl.BlockSpec((groups, PEER_HEADS, tn), lambda i, e: (e, 0, i)),
                  pl.BlockSpec((PEER_HEADS, N_KEYS, tn), lambda i, e: (0, 0, i)),
                  pl.BlockSpec((STAT_ROWS, PEER_HEADS, tn), lambda i, e: (0, 0, i))] + extra_specs,
        out_specs=pl.BlockSpec((tn, d), lambda i, e: (i, 0)),
        out_shape=jax.ShapeDtypeStruct((n, d), F32),
        scratch_shapes=[pltpu.VMEM((tn, d), BF16),
                        pltpu.VMEM((PEER_HEADS, N_KEYS, tn), F32),
                        pltpu.VMEM((PEER_EB, tn), BF16)],
        compiler_params=_params(2),
        name="peer_dense",
    )(x, g.reshape(1, d), u_tab, v_tab, s0_by_key, s1t, stats_by_kind, *extra_args)


def _conv_layer(x, nb, t, ctx_a, ctx_b, g, w_in, ca_w, ca_b, ln_g, ln_b, cb_w, w_out):
    z = _norm_matmul(x, w_in, g, name="conv_in")
    y, new_a, new_b = _conv_mix(z.reshape(nb, t, -1), ctx_a, ctx_b, ca_w, ca_b, ln_g, ln_b, cb_w)
    x = _norm_matmul(y.reshape(nb * t, -1), w_out, res=x, name="conv_out")
    return x, new_a, new_b


def _mla_project(x, pos, g, w_down, q_norm, kv_norm, w_uq_perm, w_uk_t):
    z = _norm_matmul(x, w_down, g, name="mla_down")
    q = _norm_matmul(z, w_uq_perm, q_norm, name="mla_uq")
    qlat = _mla_qlat(q, w_uk_t)
    cos, sin = _rope_tables(pos)
    c32, cbf, kr32, krbf, qrope = _mla_post(z, q, cos, sin, kv_norm)
    return qlat, qrope, c32, cbf, kr32, krbf


def _mla_finish(x, o_lat, w_uv_t, w_o):
    return _norm_matmul(_mla_uv(o_lat, w_uv_t), w_o, res=x, name="mla_o")


def _xattn_layer(x, nb, t, k, v, g, wq, wo, layer):
    d = x.shape[1]
    q = _norm_matmul(x, wq, g, layer=layer, name="xa_q")
    o = _xattn(q.reshape(nb, t, d), k, v)
    return _norm_matmul(o.reshape(nb * t, d), wo, res=x, layer=layer, name="xa_o")


def _peer_layer(x, g, w_q, layer, subkeys, u_tab, v_tab, final_g=None):
    q = _norm_matmul(x, w_q, g, layer=layer, name="peer_q")
    s0t, s1t, stats = _peer_route(q, subkeys)
    return _peer_dense(x, g, u_tab, v_tab, s0t, s1t, stats, final_g)


def kernel(x_prompt, x_sample, cache_conv_a, cache_conv_b, cache_mla_latent, cache_mla_krope, cache_mem_k, cache_mem_v, mem_prompt, norm_mix, w_conv_in, conv_a_w, conv_a_b, ln_a_g, ln_a_b, conv_b_w, w_conv_out, w_mla_down, mla_q_norm, mla_kv_norm, w_mla_uq, w_mla_uk, w_mla_uv, w_mla_o, norm_xa, norm_mem, w_xa_q, w_xa_k, w_xa_v, w_xa_o, norm_ffn, peer_w_q, peer_subkeys, peer_u, peer_v, norm_final):
    bp, tp, d = x_prompt.shape
    bs, ts, _ = x_sample.shape
    depth = norm_mix.shape[0]
    past = cache_mla_latent.shape[2]
    n_mem = mem_prompt.shape[1]
    xp = x_prompt.reshape(bp * tp, d)
    xs = x_sample.reshape(bs * ts, d)
    pos_p = jnp.tile(jnp.arange(tp, dtype=jnp.int32), bp)
    pos_s = jnp.tile(past + jnp.arange(ts, dtype=jnp.int32), bs)
    mem2d = mem_prompt.reshape(bp * n_mem, d)

    w_conv_out, w_mla_o, w_xa_q, w_xa_k, w_xa_v, w_xa_o, peer_w_q, w_conv_in = (
        w.astype(BF16) for w in (w_conv_out, w_mla_o, w_xa_q, w_xa_k, w_xa_v, w_xa_o, peer_w_q,
                                 w_conv_in))

    conv_a_p, conv_b_p, lat_p, kr_p, mem_k_p, mem_v_p = [], [], [], [], [], []
    conv_a_s, conv_b_s, lat_s, kr_s = [], [], [], []
    for l in range(depth):
        j = l // 2
        if l % 2 == 0:
            d_a = conv_a_w.shape[-1]
            zero_a = jnp.zeros((bp, CONV_A_WIDTH - 1, d_a), F32)
            zero_b = jnp.zeros((bp, CONV_B_WIDTH - 1, conv_b_w.shape[-1]), F32)
            wts = (norm_mix[l], w_conv_in[j], conv_a_w[j], conv_a_b[j], ln_a_g[j], ln_a_b[j],
                   conv_b_w[j], w_conv_out[j])
            xp, sa, sb = _conv_layer(xp, bp, tp, zero_a, zero_b, *wts)
            conv_a_p.append(sa)
            conv_b_p.append(sb)
            xs, sa, sb = _conv_layer(xs, bs, ts, cache_conv_a[j], cache_conv_b[j], *wts)
            conv_a_s.append(sa)
            conv_b_s.append(sb)
        else:
            uq = w_mla_uq[j].reshape(Q_LORA, MLA_HEADS, QK_NOPE + QK_ROPE)
            w_uq_perm = jnp.concatenate(
                [uq[:, :, :QK_NOPE].reshape(Q_LORA, MLA_HEADS * QK_NOPE),
                 uq[:, :, QK_NOPE:].reshape(Q_LORA, MLA_HEADS * QK_ROPE)], axis=1)
            w_uk_t = jnp.transpose(w_mla_uk[j], (1, 2, 0))
            w_uv_t = jnp.transpose(w_mla_uv[j], (1, 0, 2))
            proj = (norm_mix[l], w_mla_down[j], mla_q_norm[j], mla_kv_norm[j], w_uq_perm, w_uk_t)

            ql, qr, c32, cbf, kr32, krbf = _mla_project(xp, pos_p, *proj)
            o_lat = _mla_prompt_attn(ql, qr, cbf.reshape(bp, tp, KV_LORA),
                                     krbf.reshape(bp, tp, QK_ROPE))
            xp = _mla_finish(xp, o_lat, w_uv_t, w_mla_o[j])
            lat_p.append(c32.reshape(bp, tp, KV_LORA))
            kr_p.append(kr32.reshape(bp, tp, QK_ROPE))

            ql, qr, c32, cbf, kr32, krbf = _mla_project(xs, pos_s, *proj)
            c_new = c32.reshape(bs, ts, KV_LORA)
            kr_new = kr32.reshape(bs, ts, QK_ROPE)
            o_lat = _mla_decode_attn(ql, qr, cache_mla_latent[j], cache_mla_krope[j], c_new, kr_new)
            xs = _mla_finish(xs, o_lat, w_uv_t, w_mla_o[j])
            lat_s.append(c_new)
            kr_s.append(kr_new)

        mk = _norm_matmul(mem2d, w_xa_k, norm_mem[l], layer=l, name="mem_k")
        mv = _norm_matmul(mem2d, w_xa_v, norm_mem[l], layer=l, name="mem_v")
        hd = d // XA_HEADS
        mem_k_p.append(mk.reshape(bp, n_mem, XA_HEADS, hd))
        mem_v_p.append(mv.reshape(bp, n_mem, XA_HEADS, hd))
        xp = _xattn_layer(xp, bp, tp, mk.reshape(bp, n_mem, d), mv.reshape(bp, n_mem, d),
                          norm_xa[l], w_xa_q, w_xa_o, l)
        xs = _xattn_layer(xs, bs, ts, cache_mem_k[l].reshape(bs, n_mem, d),
                          cache_mem_v[l].reshape(bs, n_mem, d), norm_xa[l], w_xa_q, w_xa_o, l)

        u_tab = peer_u[l].astype(BF16)
        v_tab = peer_v[l].astype(BF16)
        final_g = norm_final if l == depth - 1 else None
        xp = _peer_layer(xp, norm_ffn[l], peer_w_q, l, peer_subkeys[l], u_tab, v_tab, final_g)
        xs = _peer_layer(xs, norm_ffn[l], peer_w_q, l, peer_subkeys[l], u_tab, v_tab, final_g)

    y_prompt = xp.reshape(bp, tp, d)
    y_sample = xs.reshape(bs, ts, d)
    return (y_prompt, y_sample,
            jnp.stack(conv_a_p), jnp.stack(conv_b_p), jnp.stack(lat_p), jnp.stack(kr_p),
            jnp.stack(mem_k_p), jnp.stack(mem_v_p),
            jnp.stack(conv_a_s), jnp.stack(conv_b_s), jnp.stack(lat_s), jnp.stack(kr_s))
```
